```python
import jax, jax.numpy as jnp
from jax import lax
import numpy as np

D_MODEL = 2048
BATCH = 4
SEQ = 8192
DEPTH = 1

CHUNK = 64
EPS = 1e-6
SSM_D_INNER = 2048
SSM_HEAD_DIM = 64
SSM_HEADS = SSM_D_INNER // SSM_HEAD_DIM
SSM_GROUPS = 4
SSM_HEADS_PER_GROUP = SSM_HEADS // SSM_GROUPS
SSM_STATE = 128
SSM_CONV = 4
SSD_CHUNK = 64
SSM_CONV_DIM = SSM_D_INNER + 2 * SSM_GROUPS * SSM_STATE
ATTN_HEADS = 16
ATTN_HEAD_DIM = 128
ATTN_D = ATTN_HEADS * ATTN_HEAD_DIM
IDX_HEADS = 16
IDX_HEAD_DIM = 64
TOPK_MAX = 256
ATTN_SCALE = ATTN_HEAD_DIM ** -0.5
IDX_SCALE = (IDX_HEAD_DIM ** -0.5) * (IDX_HEADS ** -0.5)
N_BRANCHES = 2

IN_SPLITS = (
    SSM_D_INNER,
    SSM_CONV_DIM,
    SSM_HEADS,
    ATTN_D,
    ATTN_D,
    ATTN_D,
    ATTN_D,
    IDX_HEADS * IDX_HEAD_DIM,
    IDX_HEAD_DIM,
    IDX_HEADS,
    N_BRANCHES * D_MODEL,
)
D_IN = sum(IN_SPLITS)
BRANCH_IN = SSM_D_INNER + ATTN_D

kernel_name = "hybrid_ssd_dsa_gated_block"


def rmsnorm(x, g):
    xf = x.astype(jnp.float32)
    y = xf * lax.rsqrt(jnp.mean(xf * xf, axis=-1, keepdims=True) + EPS)
    return (y * g.astype(jnp.float32)).astype(x.dtype)


def causal_dwconv(u, w, b):
    L = u.shape[1]
    up = jnp.pad(u, ((0, 0), (SSM_CONV - 1, 0), (0, 0)))
    out = up[:, 0:L] * w[0]
    for k in range(1, SSM_CONV):
        out = out + up[:, k:k + L] * w[k]
    return out + b


def ssd_scan(xh, dt, a_neg, bm, cm):
    bsz, L = xh.shape[:2]
    nc = L // SSD_CHUNK

    def to_chunks(t):
        return jnp.moveaxis(t.reshape(bsz, nc, SSD_CHUNK, *t.shape[2:]), 1, 0)

    tri = jnp.tril(jnp.ones((SSD_CHUNK, SSD_CHUNK), dtype=bool))[None, :, :, None, None]

    def step(h, inp):
        xc, dtc, bc, cc = inp
        acum = jnp.cumsum(dtc * a_neg, axis=1)
        diff = acum[:, :, None] - acum[:, None, :]
        decay = jnp.exp(jnp.where(tri, diff, -jnp.inf))
        cb = jnp.einsum('btgn,bsgn->btsg', cc, bc)
        y_diag = jnp.einsum('btsg,btsgi,bsgi,bsgip->btgip', cb, decay, dtc, xc)
        y_off = jnp.einsum('btgn,bgipn,btgi->btgip', cc, h, jnp.exp(acum))
        a_last = acum[:, -1]
        w_s = jnp.exp(a_last[:, None] - acum) * dtc
        h_new = jnp.exp(a_last)[..., None, None] * h + jnp.einsum('bsgn,bsgi,bsgip->bgipn', bc, w_s, xc)
        return h_new, y_diag + y_off

    h0 = jnp.zeros((bsz, SSM_GROUPS, SSM_HEADS_PER_GROUP, SSM_HEAD_DIM, SSM_STATE), jnp.float32)
    _, ys = lax.scan(step, h0, (to_chunks(xh), to_chunks(dt), to_chunks(bm), to_chunks(cm)))
    return jnp.moveaxis(ys, 0, 1).reshape(bsz, L, SSM_GROUPS, SSM_HEADS_PER_GROUP, SSM_HEAD_DIM)


def ssd_branch(z, xbc_raw, dt_raw, conv_w, conv_b, dt_bias, a_log, d_skip, gn_m):
    bsz, L = z.shape[:2]
    xbc = jax.nn.silu(causal_dwconv(xbc_raw, conv_w, conv_b)).astype(jnp.float32)
    gn = SSM_GROUPS * SSM_STATE
    xs = xbc[..., :SSM_D_INNER]
    bm = xbc[..., SSM_D_INNER:SSM_D_INNER + gn].reshape(bsz, L, SSM_GROUPS, SSM_STATE)
    cm = xbc[..., SSM_D_INNER + gn:].reshape(bsz, L, SSM_GROUPS, SSM_STATE)
    xh = xs.reshape(bsz, L, SSM_GROUPS, SSM_HEADS_PER_GROUP, SSM_HEAD_DIM)
    dt = jax.nn.softplus(dt_raw.astype(jnp.float32) + dt_bias.astype(jnp.float32))
    dt = dt.reshape(bsz, L, SSM_GROUPS, SSM_HEADS_PER_GROUP)
    a_neg = -jnp.exp(a_log.astype(jnp.float32)).reshape(SSM_GROUPS, SSM_HEADS_PER_GROUP)
    y = ssd_scan(xh, dt, a_neg, bm, cm)
    y = y + d_skip.astype(jnp.float32).reshape(SSM_GROUPS, SSM_HEADS_PER_GROUP)[:, :, None] * xh
    y = y.reshape(bsz, L, SSM_D_INNER) * jax.nn.silu(z.astype(jnp.float32))
    yg = y.reshape(bsz, L, SSM_GROUPS, -1)
    yg = yg * lax.rsqrt(jnp.mean(yg * yg, axis=-1, keepdims=True) + EPS)
    y = yg.reshape(bsz, L, SSM_D_INNER) * gn_m.astype(jnp.float32)
    return y.astype(z.dtype)


def dsa_branch(q, k, v, gate, q_idx, k_idx, w_idx):
    bsz, L = q.shape[:2]
    nq = L // CHUNK
    topk = min(TOPK_MAX, L // 4)
    key_pos = jnp.arange(L)

    def to_chunks(t):
        return jnp.moveaxis(t.reshape(bsz, nq, CHUNK, *t.shape[2:]), 1, 0)

    qh = q.reshape(bsz, L, ATTN_HEADS, ATTN_HEAD_DIM)
    qi = q_idx.reshape(bsz, L, IDX_HEADS, IDX_HEAD_DIM)

    def block(inp):
        c, qb, qib, wb = inp
        s = jnp.einsum('bthd,bsd->bths', qib, k_idx, preferred_element_type=jnp.float32)
        score = jnp.einsum('bths,bth->bts', jax.nn.relu(s), wb.astype(jnp.float32)) * IDX_SCALE
        visible = key_pos < (c + 1) * CHUNK
        score = jnp.where(visible[None, None, :], score, -jnp.inf)
        top_val, top_idx = lax.top_k(score, topk)
        valid = jnp.isfinite(top_val)
        kg = jax.vmap(lambda kb, ib: kb[ib])(k, top_idx)
        vg = jax.vmap(lambda vb, ib: vb[ib])(v, top_idx)
        kg = kg.reshape(bsz, CHUNK, topk, ATTN_HEADS, ATTN_HEAD_DIM)
        vg = vg.reshape(bsz, CHUNK, topk, ATTN_HEADS, ATTN_HEAD_DIM)
        logits = jnp.einsum('bthd,btkhd->bthk', qb, kg, preferred_element_type=jnp.float32) * ATTN_SCALE
        logits = jnp.where(valid[:, :, None, :], logits, -jnp.inf)
        p = jax.nn.softmax(logits, axis=-1)
        return jnp.einsum('bthk,btkhd->bthd', p.astype(vg.dtype), vg)

    outs = lax.map(block, (jnp.arange(nq), to_chunks(qh), to_chunks(qi), to_chunks(w_idx)))
    o = jnp.moveaxis(outs, 0, 1).reshape(bsz, L, ATTN_D)
    return o * jax.nn.silu(gate)


def setup_inputs(seed: int = 0) -> dict:
    key = jax.random.key(seed)
    ks = jax.random.split(key, 14)
    f32 = jnp.float32
    x = jax.random.normal(ks[0], (BATCH, SEQ, D_MODEL), f32)
    w_in = jax.random.normal(ks[1], (DEPTH, D_MODEL, D_IN), f32) * D_MODEL ** -0.5
    conv_w = jax.random.normal(ks[2], (DEPTH, SSM_CONV, SSM_CONV_DIM), f32) * SSM_CONV ** -0.5
    conv_b = jax.random.normal(ks[3], (DEPTH, SSM_CONV_DIM), f32) * 0.02
    dt0 = jnp.exp(jax.random.uniform(ks[4], (DEPTH, SSM_HEADS), f32, np.log(1e-3), np.log(1e-1)))
    dt_bias = dt0 + jnp.log(-jnp.expm1(-dt0))
    a_log = jnp.log(jax.random.uniform(ks[5], (DEPTH, SSM_HEADS), f32, 1.0, 16.0))
    d_skip = 1.0 + 0.1 * jax.random.normal(ks[6], (DEPTH, SSM_HEADS), f32)
    gn_m = 1.0 + 0.02 * jax.random.normal(ks[7], (DEPTH, SSM_D_INNER), f32)
    gate_bias = 0.02 * jax.random.normal(ks[8], (DEPTH, N_BRANCHES * D_MODEL), f32)
    w_branch = jax.random.normal(ks[9], (DEPTH, BRANCH_IN, D_MODEL), f32) * (BRANCH_IN // N_BRANCHES) ** -0.5
    w_out = jax.random.normal(ks[10], (DEPTH, D_MODEL, D_MODEL), f32) * D_MODEL ** -0.5
    norm_in = 1.0 + 0.02 * jax.random.normal(ks[11], (DEPTH, D_MODEL), f32)
    norm_final = 1.0 + 0.02 * jax.random.normal(ks[12], (D_MODEL,), f32)
    return {"x": x, "w_in": w_in, "conv_w": conv_w, "conv_b": conv_b, "dt_bias": dt_bias,
            "a_log": a_log, "d_skip": d_skip, "gn_m": gn_m, "gate_bias": gate_bias,
            "w_branch": w_branch, "w_out": w_out, "norm_in": norm_in, "norm_final": norm_final}


def reference(x, w_in, conv_w, conv_b, dt_bias, a_log, d_skip, gn_m, gate_bias,
              w_branch, w_out, norm_in, norm_final):
    offsets = list(np.cumsum(IN_SPLITS)[:-1])
    for i in range(DEPTH):
        h = rmsnorm(x, norm_in[i])
        proj = jnp.einsum('bld,de->ble', h, w_in[i])
        (z, xbc_raw, dt_raw, q, k, v, gate_a, q_idx, k_idx, w_idx, gates) = jnp.split(proj, offsets, axis=-1)
        y_m = ssd_branch(z, xbc_raw, dt_raw, conv_w[i], conv_b[i], dt_bias[i], a_log[i], d_skip[i], gn_m[i])
        y_a = dsa_branch(q, k, v, gate_a, q_idx, k_idx, w_idx)
        o_m = jnp.einsum('ble,ed->bld', y_m, w_branch[i, :SSM_D_INNER])
        o_a = jnp.einsum('ble,ed->bld', y_a, w_branch[i, SSM_D_INNER:])
        g = jax.nn.sigmoid(gates + gate_bias[i])
        merged = g[..., :D_MODEL] * o_m + g[..., D_MODEL:] * o_a
        x = x + jnp.einsum('bld,de->ble', merged, w_out[i])
    return rmsnorm(x, norm_final)
```

```python
import functools

import jax
import jax.numpy as jnp
from jax import lax
from jax.experimental import pallas as pl
from jax.experimental.pallas import tpu as pltpu

F32 = jnp.float32
BF16 = jnp.bfloat16
I32 = jnp.int32

EPS = 1e-6
CHUNK = 64
SSM_HEAD_DIM = 64
SSM_GROUPS = 4
SSM_STATE = 128
SSM_CONV = 4
ATTN_HEAD_DIM = 128
IDX_HEADS = 16
IDX_HEAD_DIM = 64
TOPK_MAX = 256
N_BRANCHES = 2

LANES = 128
NEG_BIG = -1e30
INT_MIN = -2 ** 31

VMEM_LIMIT = 56 * 1024 * 1024


def _cparams(sem):
    return pltpu.CompilerParams(dimension_semantics=sem, vmem_limit_bytes=VMEM_LIMIT)


def _rmsnorm_kernel(x_ref, g_ref, o_ref):
    x = x_ref[...]
    ms = jnp.mean(x * x, axis=-1, keepdims=True)
    o_ref[...] = (x * lax.rsqrt(ms + EPS) * g_ref[...]).astype(o_ref.dtype)


def _rmsnorm(x2d, g, out_dtype, tm=512):
    t, d = x2d.shape
    return pl.pallas_call(
        _rmsnorm_kernel,
        out_shape=jax.ShapeDtypeStruct((t, d), out_dtype),
        grid=(t // tm,),
        in_specs=[pl.BlockSpec((tm, d), lambda i: (i, 0)),
                  pl.BlockSpec((1, d), lambda i: (0, 0))],
        out_specs=pl.BlockSpec((tm, d), lambda i: (i, 0)),
        compiler_params=_cparams(("parallel",)),
        name="rmsnorm",
    )(x2d, g.reshape(1, d))


def _matmul_nn_kernel(a_ref, b_ref, o_ref):
    o_ref[...] = jnp.dot(a_ref[...], b_ref[...], preferred_element_type=F32).astype(o_ref.dtype)


def _matmul_nn(a, b, out_dtype, tm, tn, name):
    t, k = a.shape
    n = b.shape[1]
    return pl.pallas_call(
        _matmul_nn_kernel,
        out_shape=jax.ShapeDtypeStruct((t, n), out_dtype),
        grid=(t // tm, n // tn),
        in_specs=[pl.BlockSpec((tm, k), lambda i, j: (i, 0)),
                  pl.BlockSpec((k, tn), lambda i, j: (0, j))],
        out_specs=pl.BlockSpec((tm, tn), lambda i, j: (i, j)),
        compiler_params=_cparams(("parallel", "arbitrary")),
        name=name,
    )(a, b)


def _matmul_nt_kernel(w_ref, a_ref, o_ref, *, tt):
    res = lax.dot_general(w_ref[...], a_ref[...], (((1,), (1,)), ((), ())),
                          preferred_element_type=F32).astype(o_ref.dtype)
    for c in range(o_ref.shape[0]):
        o_ref[c] = res[:, c * tt:(c + 1) * tt]


def _matmul_nt(wt, a, out_dtype, tm, tn, tt, name):
    t, k = a.shape
    n = wt.shape[0]
    return pl.pallas_call(
        functools.partial(_matmul_nt_kernel, tt=tt),
        out_shape=jax.ShapeDtypeStruct((t // tt, n, tt), out_dtype),
        grid=(t // tm, n // tn),
        in_specs=[pl.BlockSpec((tn, k), lambda i, j: (j, 0)),
                  pl.BlockSpec((tm, k), lambda i, j: (i, 0))],
        out_specs=pl.BlockSpec((tm // tt, tn, tt), lambda i, j: (i, j, 0)),
        compiler_params=_cparams(("parallel", "arbitrary")),
        name=name,
    )(wt, a)


def _softplus(x):
    return jnp.maximum(x, 0.0) + jnp.log1p(jnp.exp(-jnp.abs(x)))


def _silu(x):
    return x * jax.nn.sigmoid(x)


def _ssd_kernel(zx_ref, dt_ref, convw_ref, convb_ref, dtb_ref, aneg_ref, dskip_ref, gnm_ref,
                o_ref, ext_ref, xs_ref, y_ref, state_ref, *, q, d_inner, conv_dim):
    n_heads = d_inner // SSM_HEAD_DIM
    n_pairs = n_heads // 2
    pairs_per_group = n_pairs // SSM_GROUPS
    halo = 8

    @pl.when(pl.program_id(1) == 0)
    def _():
        ext_ref[0:halo, :] = jnp.zeros((halo, conv_dim), F32)
        state_ref[...] = jnp.zeros(state_ref.shape, F32)

    ext_ref[halo:halo + q, :] = zx_ref[:, d_inner:d_inner + conv_dim]
    ct = 512
    for c0 in range(0, conv_dim, ct):
        base = halo - (SSM_CONV - 1)
        acc = ext_ref[base:base + q, c0:c0 + ct] * convw_ref[0:1, c0:c0 + ct]
        for kk in range(1, SSM_CONV):
            acc = acc + ext_ref[base + kk:base + kk + q, c0:c0 + ct] * convw_ref[kk:kk + 1, c0:c0 + ct]
        xs_ref[:, c0:c0 + ct] = _silu(acc + convb_ref[:, c0:c0 + ct])
    ext_ref[0:halo, :] = ext_ref[q:q + halo, :]

    dt = _softplus(dt_ref[...] + dtb_ref[...])
    a = dt * aneg_ref[...]
    row_i = lax.broadcasted_iota(I32, (q, q), 0)
    col_i = lax.broadcasted_iota(I32, (q, q), 1)
    tril = row_i >= col_i
    acum = jnp.dot(tril.astype(F32), a, precision=lax.Precision.HIGHEST,
                   preferred_element_type=F32)
    acum_t = acum.T
    a_last = acum[q - 1:q, :]
    e_in = jnp.exp(acum)
    e_out = jnp.exp(a_last - acum)
    e_last = jnp.exp(a_last)

    lane_lo = lax.broadcasted_iota(I32, (q, LANES), 1) < SSM_HEAD_DIM
    lane_lo_row = lax.broadcasted_iota(I32, (1, LANES), 1) < SSM_HEAD_DIM
    bc_off = d_inner
    cc_off = d_inner + SSM_GROUPS * SSM_STATE

    for g in range(SSM_GROUPS):
        bg = xs_ref[:, bc_off + g * SSM_STATE:bc_off + (g + 1) * SSM_STATE]
        cg = xs_ref[:, cc_off + g * SSM_STATE:cc_off + (g + 1) * SSM_STATE]
        bg16 = bg.astype(BF16)
        cg16 = cg.astype(BF16)
        cb = lax.dot_general(cg16, bg16, (((1,), (1,)), ((), ())), preferred_element_type=F32)
        bgt16 = bg.T.astype(BF16)
        for pi in range(pairs_per_group):
            pp = g * pairs_per_group + pi
            h0, h1 = 2 * pp, 2 * pp + 1
            x_pair = xs_ref[:, pp * LANES:(pp + 1) * LANES]
            xdt = x_pair * jnp.where(lane_lo, dt[:, h0:h0 + 1], dt[:, h1:h1 + 1])
            y_pair = dskip_ref[:, pp * LANES:(pp + 1) * LANES] * x_pair
            for hh, keep in ((h0, lane_lo), (h1, jnp.logical_not(lane_lo))):
                diff = acum[:, hh:hh + 1] - acum_t[hh:hh + 1, :]
                dec = jnp.exp(jnp.where(tril, diff, NEG_BIG))
                m16 = (cb * dec).astype(BF16)
                xm16 = jnp.where(keep, xdt, 0.0).astype(BF16)
                y_pair = y_pair + jnp.dot(m16, xm16, preferred_element_type=F32)
            st = state_ref[pp]
            e_in_pair = jnp.where(lane_lo, e_in[:, h0:h0 + 1], e_in[:, h1:h1 + 1])
            y_pair = y_pair + jnp.dot(cg16, st.astype(BF16), preferred_element_type=F32) * e_in_pair
            e_out_pair = jnp.where(lane_lo, e_out[:, h0:h0 + 1], e_out[:, h1:h1 + 1])
            wx16 = (xdt * e_out_pair).astype(BF16)
            e_last_pair = jnp.where(lane_lo_row, e_last[:, h0:h0 + 1], e_last[:, h1:h1 + 1])
            state_ref[pp] = st * e_last_pair + jnp.dot(bgt16, wx16, preferred_element_type=F32)
            y_ref[:, pp * LANES:(pp + 1) * LANES] = y_pair

    gw = d_inner // SSM_GROUPS
    for g in range(SSM_GROUPS):
        z = zx_ref[:, g * gw:(g + 1) * gw]
        yg = y_ref[:, g * gw:(g + 1) * gw] * _silu(z)
        ms = jnp.mean(yg * yg, axis=-1, keepdims=True)
        o_ref[:, g * gw:(g + 1) * gw] = (yg * lax.rsqrt(ms + EPS) * gnm_ref[:, g * gw:(g + 1) * gw]).astype(o_ref.dtype)


def _ssd_branch(zx, dtk, conv_w, conv_b, dt_bias, a_log, d_skip, gn_m, bsz, seq, q=256):
    t = zx.shape[0]
    d_inner = gn_m.shape[0]
    conv_dim = conv_w.shape[1]
    n_heads = d_inner // SSM_HEAD_DIM
    nc = seq // q
    pad = LANES - n_heads
    dtb = jnp.pad(dt_bias.astype(F32), (0, pad)).reshape(1, LANES)
    aneg = jnp.pad(-jnp.exp(a_log.astype(F32)), (0, pad)).reshape(1, LANES)
    dskip = jnp.repeat(d_skip.astype(F32), SSM_HEAD_DIM).reshape(1, d_inner)
    const = lambda b, c: (0, 0)
    return pl.pallas_call(
        functools.partial(_ssd_kernel, q=q, d_inner=d_inner, conv_dim=conv_dim),
        out_shape=jax.ShapeDtypeStruct((t, d_inner), BF16),
        grid=(bsz, nc),
        in_specs=[pl.BlockSpec((q, d_inner + conv_dim), lambda b, c: (b * nc + c, 0)),
                  pl.BlockSpec((q, LANES), lambda b, c: (b * nc + c, 0)),
                  pl.BlockSpec((SSM_CONV, conv_dim), const),
                  pl.BlockSpec((1, conv_dim), const),
                  pl.BlockSpec((1, LANES), const),
                  pl.BlockSpec((1, LANES), const),
                  pl.BlockSpec((1, d_inner), const),
                  pl.BlockSpec((1, d_inner), const)],
        out_specs=pl.BlockSpec((q, d_inner), lambda b, c: (b * nc + c, 0)),
        scratch_shapes=[pltpu.VMEM((q + 8, conv_dim), F32),
                        pltpu.VMEM((q, conv_dim), F32),
                        pltpu.VMEM((q, d_inner), F32),
                        pltpu.VMEM((n_heads // 2, SSM_STATE, LANES), F32)],
        compiler_params=_cparams(("parallel", "arbitrary")),
        name="ssd_branch",
    )(zx, dtk, conv_w.astype(F32), conv_b.astype(F32).reshape(1, conv_dim), dtb, aneg, dskip,
      gn_m.astype(F32).reshape(1, d_inner))


def _select_kernel(qi_ref, w_ref, kidx_ref, bias_ref, keys_ref, *, tq, tk, topk, idx_scale):
    seq = kidx_ref.shape[0]
    j = pl.program_id(1)
    n_tiles = ((j + 1) * tq + tk - 1) // tk
    lane = lax.broadcasted_iota(I32, (1, tq), 1)
    limit = ((j * tq + lane) // CHUNK + 1) * CHUNK
    row_local = lax.broadcasted_iota(I32, (tk, tq), 0)

    def score_tile(kt, carry):
        r0 = pl.multiple_of(kt * tk, tk)
        kb = kidx_ref[pl.ds(r0, tk), :]
        acc = jnp.zeros((tk, tq), F32)
        for h in range(IDX_HEADS):
            s = jnp.dot(kb, qi_ref[h * IDX_HEAD_DIM:(h + 1) * IDX_HEAD_DIM, :],
                        preferred_element_type=F32)
            acc = acc + jnp.maximum(s, 0.0) * w_ref[h:h + 1, :]
        sc = acc * idx_scale + 0.0
        bits = lax.bitcast_convert_type(sc, I32)
        key = jnp.where(bits < 0, bits ^ 0x7FFFFFFF, bits)
        key = jnp.where(row_local + r0 < limit, key, INT_MIN)
        keys_ref[pl.ds(r0, tk), :] = key
        return carry

    lax.fori_loop(0, n_tiles, score_tile, 0)

    def count(pred):
        def body(kt, acc):
            r0 = pl.multiple_of(kt * tk, tk)
            ones = jnp.where(pred(keys_ref[pl.ds(r0, tk), :], r0), 1, 0).astype(I32)
            return acc + jnp.sum(ones.reshape(tk // 8, 8, tq), axis=0)
        acc = lax.fori_loop(0, n_tiles, body, jnp.zeros((8, tq), I32))
        return jnp.sum(acc, axis=0, keepdims=True)

    def bisect(i, thr):
        cand = thr + jnp.left_shift(jnp.int32(1), 31 - i)
        c = count(lambda kk, r0: kk >= cand)
        return jnp.where(c >= topk, cand, thr)

    thr = lax.fori_loop(0, 32, bisect, jnp.full((1, tq), INT_MIN, I32))

    c_ge = count(lambda kk, r0: kk >= thr)
    c_gt = count(lambda kk, r0: kk > thr)
    need = topk - c_gt
    real = thr > INT_MIN
    tie_excess = jnp.max(jnp.where(jnp.logical_and(real, c_ge > topk), 1, 0)) > 0
    idx_bits = seq.bit_length()

    def tie_limit():
        def bisect_idx(i, lim):
            cand = lim + jnp.left_shift(jnp.int32(1), idx_bits - 1 - i)
            c = count(lambda kk, r0: jnp.logical_and(kk == thr, row_local + r0 < cand))
            return jnp.where(c <= need, cand, lim)
        return lax.fori_loop(0, idx_bits, bisect_idx, jnp.zeros((1, tq), I32))

    lim = lax.cond(tie_excess, tie_limit, lambda: jnp.full((1, tq), 2 ** idx_bits - 1, I32))
    lim = jnp.where(real, lim, 0)

    def write_tile(kt, carry):
        r0 = pl.multiple_of(kt * tk, tk)
        kk = keys_ref[pl.ds(r0, tk), :]
        sel = jnp.logical_or(kk > thr, jnp.logical_and(kk == thr, row_local + r0 < lim))
        bias_ref[pl.ds(r0, tk), :] = jnp.where(sel, 0.0, NEG_BIG).astype(bias_ref.dtype)
        return carry

    lax.fori_loop(0, n_tiles, write_tile, 0)

    def fill_tile(kt, carry):
        r0 = pl.multiple_of(kt * tk, tk)
        bias_ref[pl.ds(r0, tk), :] = jnp.full((tk, tq), NEG_BIG, bias_ref.dtype)
        return carry

    lax.fori_loop(n_tiles, seq // tk, fill_tile, 0)


def _select_bias(qi_t, w_t, kidx, bsz, seq, tq, tk):
    nq = seq // tq
    topk = min(TOPK_MAX, seq // 4)
    idx_scale = (IDX_HEAD_DIM ** -0.5) * (IDX_HEADS ** -0.5)
    n_qi = qi_t.shape[1]
    return pl.pallas_call(
        functools.partial(_select_kernel, tq=tq, tk=tk, topk=topk, idx_scale=idx_scale),
        out_shape=jax.ShapeDtypeStruct((bsz, seq, seq), BF16),
        grid=(bsz, nq),
        in_specs=[pl.BlockSpec((None, n_qi, tq), lambda b, j: (b * nq + j, 0, 0)),
                  pl.BlockSpec((None, IDX_HEADS, tq), lambda b, j: (b * nq + j, 0, 0)),
                  pl.BlockSpec((None, seq, IDX_HEAD_DIM), lambda b, j: (b, 0, 0))],
        out_specs=pl.BlockSpec((None, seq, tq), lambda b, j: (b, 0, j)),
        scratch_shapes=[pltpu.VMEM((seq, tq), I32)],
        compiler_params=_cparams(("parallel", "arbitrary")),
        name="topk_select",
    )(qi_t, w_t, kidx)


def _attn_kernel(q_ref, k_ref, v_ref, bias_ref, gate_ref, o_ref, *, tq, tk, heads, scale):
    j = pl.program_id(2)
    n_tiles = ((j + 1) * tq + tk - 1) // tk
    hd = ATTN_HEAD_DIM
    outs = []
    for i in range(heads):
        qh = q_ref[i * hd:(i + 1) * hd, :]

        def body(kt, carry, i=i, qh=qh):
            m, l, acc = carry
            r0 = pl.multiple_of(kt * tk, tk)
            kb = k_ref[pl.ds(r0, tk), i * hd:(i + 1) * hd]
            s = jnp.dot(kb, qh, preferred_element_type=F32) * scale
            s = s + bias_ref[pl.ds(r0, tk), :].astype(F32)
            m_new = jnp.maximum(m, jnp.max(s, axis=0, keepdims=True))
            p = jnp.exp(s - m_new)
            alpha = jnp.exp(m - m_new)
            l_new = alpha * l + jnp.sum(p, axis=0, keepdims=True)
            vb = v_ref[kt, i * hd:(i + 1) * hd, :]
            acc_new = alpha * acc + jnp.dot(vb, p.astype(BF16), preferred_element_type=F32)
            return m_new, l_new, acc_new

        init = (jnp.full((1, tq), NEG_BIG, F32), jnp.zeros((1, tq), F32), jnp.zeros((hd, tq), F32))
        m, l, acc = lax.fori_loop(0, n_tiles, body, init)
        outs.append(acc / l)
    o = jnp.concatenate(outs, axis=0).T
    o_ref[...] = (o * _silu(gate_ref[...])).astype(o_ref.dtype)


def _attention(q_t, k, v_t, bias, gates, bsz, seq, tq, tk, heads=2):
    nq = seq // tq
    nkt = seq // tk
    d_attn = k.shape[2]
    hw = heads * ATTN_HEAD_DIM
    scale = ATTN_HEAD_DIM ** -0.5
    return pl.pallas_call(
        functools.partial(_attn_kernel, tq=tq, tk=tk, heads=heads, scale=scale),
        out_shape=jax.ShapeDtypeStruct((bsz * seq, d_attn), BF16),
        grid=(bsz, d_attn // hw, nq),
        in_specs=[pl.BlockSpec((None, hw, tq), lambda b, h, j: (b * nq + j, h, 0)),
                  pl.BlockSpec((None, seq, hw), lambda b, h, j: (b, 0, h)),
                  pl.BlockSpec((nkt, hw, tk), lambda b, h, j: (b, h, 0)),
                  pl.BlockSpec((None, seq, tq), lambda b, h, j: (b, 0, j)),
                  pl.BlockSpec((tq, hw), lambda b, h, j: (b * nq + j, h))],
        out_specs=pl.BlockSpec((tq, hw), lambda b, h, j: (b * nq + j, h)),
        compiler_params=_cparams(("parallel", "parallel", "arbitrary")),
        name="masked_attention",
    )(q_t, k, v_t, bias, gates)


def _merge_kernel(ym_ref, ya_ref, wm_ref, wa_ref, gm_ref, ga_ref, bm_ref, ba_ref, o_ref):
    o_m = jnp.dot(ym_ref[...], wm_ref[...], preferred_element_type=F32)
    o_a = jnp.dot(ya_ref[...], wa_ref[...], preferred_element_type=F32)
    g_m = jax.nn.sigmoid(gm_ref[...] + bm_ref[...])
    g_a = jax.nn.sigmoid(ga_ref[...] + ba_ref[...])
    o_ref[...] = (g_m * o_m + g_a * o_a).astype(o_ref.dtype)


def _merge(y_m, y_a, w_m, w_a, gates, gate_bias, d_attn, tm=1024, tn=512):
    t, d_inner = y_m.shape
    d_model = w_m.shape[1]
    nb = d_model // tn
    off_m = d_attn // tn
    off_a = off_m + nb
    gb = gate_bias.astype(F32).reshape(1, N_BRANCHES * d_model)
    return pl.pallas_call(
        _merge_kernel,
        out_shape=jax.ShapeDtypeStruct((t, d_model), BF16),
        grid=(t // tm, nb),
        in_specs=[pl.BlockSpec((tm, d_inner), lambda i, j: (i, 0)),
                  pl.BlockSpec((tm, d_attn), lambda i, j: (i, 0)),
                  pl.BlockSpec((d_inner, tn), lambda i, j: (0, j)),
                  pl.BlockSpec((d_attn, tn), lambda i, j: (0, j)),
                  pl.BlockSpec((tm, tn), lambda i, j: (i, off_m + j)),
                  pl.BlockSpec((tm, tn), lambda i, j: (i, off_a + j)),
                  pl.BlockSpec((1, tn), lambda i, j: (0, j)),
                  pl.BlockSpec((1, tn), lambda i, j: (0, nb + j))],
        out_specs=pl.BlockSpec((tm, tn), lambda i, j: (i, j)),
        compiler_params=_cparams(("parallel", "arbitrary")),
        name="branch_merge",
    )(y_m, y_a, w_m, w_a, gates, gates, gb, gb)


def _out_kernel(m_ref, w_ref, x_ref, o_ref):
    o_ref[...] = x_ref[...] + jnp.dot(m_ref[...], w_ref[...], preferred_element_type=F32)


def _out_norm_kernel(m_ref, w_ref, x_ref, g_ref, o_ref):
    y = x_ref[...] + jnp.dot(m_ref[...], w_ref[...], preferred_element_type=F32)
    ms = jnp.mean(y * y, axis=-1, keepdims=True)
    o_ref[...] = y * lax.rsqrt(ms + EPS) * g_ref[...]


def _out_proj(merged, w_out, x2d, norm_g, tm=512):
    t, d = x2d.shape
    row = pl.BlockSpec((tm, d), lambda i: (i, 0))
    in_specs = [row, pl.BlockSpec((d, d), lambda i: (0, 0)), row]
    args = [merged, w_out, x2d]
    kern = _out_kernel
    if norm_g is not None:
        in_specs.append(pl.BlockSpec((1, d), lambda i: (0, 0)))
        args.append(norm_g.astype(F32).reshape(1, d))
        kern = _out_norm_kernel
    return pl.pallas_call(
        kern,
        out_shape=jax.ShapeDtypeStruct((t, d), F32),
        grid=(t // tm,),
        in_specs=in_specs,
        out_specs=row,
        compiler_params=_cparams(("parallel",)),
        name="out_proj",
    )(*args)


def kernel(x, w_in, conv_w, conv_b, dt_bias, a_log, d_skip, gn_m, gate_bias, w_branch, w_out,
           norm_in, norm_final):
    bsz, seq, d_model = x.shape
    depth = w_in.shape[0]
    d_inner = gn_m.shape[1]
    conv_dim = conv_w.shape[2]
    n_heads = dt_bias.shape[1]
    d_attn = w_branch.shape[1] - d_inner
    n_qi = IDX_HEADS * IDX_HEAD_DIM
    t = bsz * seq
    tq, tk = 128, 512
    tm = 1024

    o_z = 0
    o_xbc = o_z + d_inner
    o_dt = o_xbc + conv_dim
    o_q = o_dt + n_heads
    o_k = o_q + d_attn
    o_v = o_k + d_attn
    o_ga = o_v + d_attn
    o_qi = o_ga + d_attn
    o_ki = o_qi + n_qi
    o_wi = o_ki + IDX_HEAD_DIM
    o_g = o_wi + IDX_HEADS
    small_pad = LANES - n_heads - IDX_HEAD_DIM

    x2d = x.reshape(t, d_model)
    for i in range(depth):
        w = w_in[i].astype(BF16)
        w_zx = w[:, o_z:o_dt]
        w_small = jnp.concatenate([w[:, o_dt:o_q], w[:, o_ki:o_wi],
                                   jnp.zeros((d_model, small_pad), BF16)], axis=1)
        w_k = w[:, o_k:o_v]
        w_gates = jnp.concatenate([w[:, o_ga:o_qi], w[:, o_g:]], axis=1)
        wt_q = w[:, o_q:o_k].T
        wt_v = w[:, o_v:o_ga].T
        wt_qi = w[:, o_qi:o_ki].T
        wt_wi = w[:, o_wi:o_g].T

        h = _rmsnorm(x2d, norm_in[i].astype(F32), BF16)
        zx = _matmul_nn(h, w_zx, F32, tm, 512, "proj_zx")
        small = _matmul_nn(h, w_small, F32, tm, LANES, "proj_small")
        k_tok = _matmul_nn(h, w_k, BF16, tm, 512, "proj_k")
        gates = _matmul_nn(h, w_gates, F32, tm, 512, "proj_gates")
        q_t = _matmul_nt(wt_q, h, BF16, tm, 512, tq, "proj_qT")
        v_t = _matmul_nt(wt_v, h, BF16, tm, 512, tk, "proj_vT")
        qi_t = _matmul_nt(wt_qi, h, BF16, tm, 512, tq, "proj_qidxT")
        wi_t = _matmul_nt(wt_wi, h, F32, tm, IDX_HEADS, tq, "proj_widxT")

        y_m = _ssd_branch(zx, small, conv_w[i], conv_b[i], dt_bias[i], a_log[i], d_skip[i], gn_m[i],
                          bsz, seq)

        kidx = small[:, n_heads:n_heads + IDX_HEAD_DIM].astype(BF16).reshape(bsz, seq, IDX_HEAD_DIM)
        bias = _select_bias(qi_t, wi_t, kidx, bsz, seq, tq, tk)
        y_a = _attention(q_t, k_tok.reshape(bsz, seq, d_attn), v_t, bias, gates, bsz, seq, tq, tk)

        wb = w_branch[i].astype(BF16)
        merged = _merge(y_m, y_a, wb[:d_inner], wb[d_inner:], gates, gate_bias[i], d_attn)
        last = i == depth - 1
        x2d = _out_proj(merged, w_out[i].astype(BF16), x2d, norm_final if last else None)
    return x2d.reshape(bsz, seq, d_model)
```

```python
import functools

import jax
import jax.numpy as jnp
from jax import lax
from jax.experimental import pallas as pl
from jax.experimental.pallas import tpu as pltpu

F32 = jnp.float32
BF16 = jnp.bfloat16
I32 = jnp.int32

EPS = 1e-6
CHUNK = 64
SSM_HEAD_DIM = 64
SSM_GROUPS = 4
SSM_STATE = 128
SSM_CONV = 4
ATTN_HEAD_DIM = 128
IDX_HEADS = 16
IDX_HEAD_DIM = 64
TOPK_MAX = 256
N_BRANCHES = 2

LANES = 128
NEG_BIG = -1e30
INT_MIN = -2 ** 31

VMEM_LIMIT = 56 * 1024 * 1024


def _cparams(sem):
    return pltpu.CompilerParams(dimension_semantics=sem, vmem_limit_bytes=VMEM_LIMIT)


def _rmsnorm_kernel(x_ref, g_ref, o_ref):
    x = x_ref[...]
    ms = jnp.mean(x * x, axis=-1, keepdims=True)
    o_ref[...] = (x * lax.rsqrt(ms + EPS) * g_ref[...]).astype(o_ref.dtype)


def _rmsnorm(x2d, g, out_dtype, tm=512):
    t, d = x2d.shape
    return pl.pallas_call(
        _rmsnorm_kernel,
        out_shape=jax.ShapeDtypeStruct((t, d), out_dtype),
        grid=(t // tm,),
        in_specs=[pl.BlockSpec((tm, d), lambda i: (i, 0)),
                  pl.BlockSpec((1, d), lambda i: (0, 0))],
        out_specs=pl.BlockSpec((tm, d), lambda i: (i, 0)),
        compiler_params=_cparams(("parallel",)),
        name="rmsnorm",
    )(x2d, g.reshape(1, d))


def _matmul_nn_kernel(a_ref, b_ref, o_ref):
    o_ref[...] = jnp.dot(a_ref[...], b_ref[...], preferred_element_type=F32).astype(o_ref.dtype)


def _matmul_nn(a, b, out_dtype, tm, tn, name):
    t, k = a.shape
    n = b.shape[1]
    return pl.pallas_call(
        _matmul_nn_kernel,
        out_shape=jax.ShapeDtypeStruct((t, n), out_dtype),
        grid=(t // tm, n // tn),
        in_specs=[pl.BlockSpec((tm, k), lambda i, j: (i, 0)),
                  pl.BlockSpec((k, tn), lambda i, j: (0, j))],
        out_specs=pl.BlockSpec((tm, tn), lambda i, j: (i, j)),
        compiler_params=_cparams(("parallel", "arbitrary")),
        name=name,
    )(a, b)


def _matmul_nt_kernel(w_ref, a_ref, o_ref, *, tt):
    res = lax.dot_general(w_ref[...], a_ref[...], (((1,), (1,)), ((), ())),
                          preferred_element_type=F32).astype(o_ref.dtype)
    for c in range(o_ref.shape[0]):
        o_ref[c] = res[:, c * tt:(c + 1) * tt]


def _matmul_nt(wt, a, out_dtype, tm, tn, tt, name):
    t, k = a.shape
    n = wt.shape[0]
    return pl.pallas_call(
        functools.partial(_matmul_nt_kernel, tt=tt),
        out_shape=jax.ShapeDtypeStruct((t // tt, n, tt), out_dtype),
        grid=(t // tm, n // tn),
        in_specs=[pl.BlockSpec((tn, k), lambda i, j: (j, 0)),
                  pl.BlockSpec((tm, k), lambda i, j: (i, 0))],
        out_specs=pl.BlockSpec((tm // tt, tn, tt), lambda i, j: (i, j, 0)),
        compiler_params=_cparams(("parallel", "arbitrary")),
        name=name,
    )(wt, a)


def _softplus(x):
    return jnp.maximum(x, 0.0) + jnp.log1p(jnp.exp(-jnp.abs(x)))


def _silu(x):
    return x * jax.nn.sigmoid(x)


def _ssd_kernel(zx_ref, dt_ref, convw_ref, convb_ref, dtb_ref, aneg_ref, dskip_ref, gnm_ref,
                o_ref, ext_ref, xs_ref, y_ref, state_ref, *, q, d_inner, conv_dim):
    n_heads = d_inner // SSM_HEAD_DIM
    n_pairs = n_heads // 2
    pairs_per_group = n_pairs // SSM_GROUPS
    halo = 8

    @pl.when(pl.program_id(1) == 0)
    def _():
        ext_ref[0:halo, :] = jnp.zeros((halo, conv_dim), F32)
        state_ref[...] = jnp.zeros(state_ref.shape, F32)

    ext_ref[halo:halo + q, :] = zx_ref[:, d_inner:d_inner + conv_dim]
    ct = 512
    for c0 in range(0, conv_dim, ct):
        base = halo - (SSM_CONV - 1)
        acc = ext_ref[base:base + q, c0:c0 + ct] * convw_ref[0:1, c0:c0 + ct]
        for kk in range(1, SSM_CONV):
            acc = acc + ext_ref[base + kk:base + kk + q, c0:c0 + ct] * convw_ref[kk:kk + 1, c0:c0 + ct]
        xs_ref[:, c0:c0 + ct] = _silu(acc + convb_ref[:, c0:c0 + ct])
    ext_ref[0:halo, :] = ext_ref[q:q + halo, :]

    dt = _softplus(dt_ref[...] + dtb_ref[...])
    a = dt * aneg_ref[...]
    row_i = lax.broadcasted_iota(I32, (q, q), 0)
    col_i = lax.broadcasted_iota(I32, (q, q), 1)
    tril = row_i >= col_i
    acum = jnp.dot(tril.astype(F32), a, precision=lax.Precision.HIGHEST,
                   preferred_element_type=F32)
    acum_t = acum.T
    a_last = acum[q - 1:q, :]
    e_in = jnp.exp(acum)
    e_out = jnp.exp(a_last - acum)
    e_last = jnp.exp(a_last)

    lane_lo = lax.broadcasted_iota(I32, (q, LANES), 1) < SSM_HEAD_DIM
    lane_lo_row = lax.broadcasted_iota(I32, (1, LANES), 1) < SSM_HEAD_DIM
    bc_off = d_inner
    cc_off = d_inner + SSM_GROUPS * SSM_STATE

    for g in range(SSM_GROUPS):
        bg = xs_ref[:, bc_off + g * SSM_STATE:bc_off + (g + 1) * SSM_STATE]
        cg = xs_ref[:, cc_off + g * SSM_STATE:cc_off + (g + 1) * SSM_STATE]
        bg16 = bg.astype(BF16)
        cg16 = cg.astype(BF16)
        cb = lax.dot_general(cg16, bg16, (((1,), (1,)), ((), ())), preferred_element_type=F32)
        bgt16 = bg.T.astype(BF16)
        for pi in range(pairs_per_group):
            pp = g * pairs_per_group + pi
            h0, h1 = 2 * pp, 2 * pp + 1
            x_pair = xs_ref[:, pp * LANES:(pp + 1) * LANES]
            xdt = x_pair * jnp.where(lane_lo, dt[:, h0:h0 + 1], dt[:, h1:h1 + 1])
            y_pair = dskip_ref[:, pp * LANES:(pp + 1) * LANES] * x_pair
            for hh, keep in ((h0, lane_lo), (h1, jnp.logical_not(lane_lo))):
                diff = acum[:, hh:hh + 1] - acum_t[hh:hh + 1, :]
                dec = jnp.exp(jnp.where(tril, diff, NEG_BIG))
                m16 = (cb * dec).astype(BF16)
                xm16 = jnp.where(keep, xdt, 0.0).astype(BF16)
                y_pair = y_pair + jnp.dot(m16, xm16, preferred_element_type=F32)
            st = state_ref[pp]
            e_in_pair = jnp.where(lane_lo, e_in[:, h0:h0 + 1], e_in[:, h1:h1 + 1])
            y_pair = y_pair + jnp.dot(cg16, st.astype(BF16), preferred_element_type=F32) * e_in_pair
            e_out_pair = jnp.where(lane_lo, e_out[:, h0:h0 + 1], e_out[:, h1:h1 + 1])
            wx16 = (xdt * e_out_pair).astype(BF16)
            e_last_pair = jnp.where(lane_lo_row, e_last[:, h0:h0 + 1], e_last[:, h1:h1 + 1])
            state_ref[pp] = st * e_last_pair + jnp.dot(bgt16, wx16, preferred_element_type=F32)
            y_ref[:, pp * LANES:(pp + 1) * LANES] = y_pair

    gw = d_inner // SSM_GROUPS
    for g in range(SSM_GROUPS):
        z = zx_ref[:, g * gw:(g + 1) * gw]
        yg = y_ref[:, g * gw:(g + 1) * gw] * _silu(z)
        ms = jnp.mean(yg * yg, axis=-1, keepdims=True)
        o_ref[:, g * gw:(g + 1) * gw] = (yg * lax.rsqrt(ms + EPS) * gnm_ref[:, g * gw:(g + 1) * gw]).astype(o_ref.dtype)


def _ssd_branch(zx, dtk, conv_w, conv_b, dt_bias, a_log, d_skip, gn_m, bsz, seq, q=256):
    t = zx.shape[0]
    d_inner = gn_m.shape[0]
    conv_dim = conv_w.shape[1]
    n_heads = d_inner // SSM_HEAD_DIM
    nc = seq // q
    pad = LANES - n_heads
    dtb = jnp.pad(dt_bias.astype(F32), (0, pad)).reshape(1, LANES)
    aneg = jnp.pad(-jnp.exp(a_log.astype(F32)), (0, pad)).reshape(1, LANES)
    dskip = jnp.repeat(d_skip.astype(F32), SSM_HEAD_DIM).reshape(1, d_inner)
    const = lambda b, c: (0, 0)
    return pl.pallas_call(
        functools.partial(_ssd_kernel, q=q, d_inner=d_inner, conv_dim=conv_dim),
        out_shape=jax.ShapeDtypeStruct((t, d_inner), BF16),
        grid=(bsz, nc),
        in_specs=[pl.BlockSpec((q, d_inner + conv_dim), lambda b, c: (b * nc + c, 0)),
                  pl.BlockSpec((q, LANES), lambda b, c: (b * nc + c, 0)),
                  pl.BlockSpec((SSM_CONV, conv_dim), const),
                  pl.BlockSpec((1, conv_dim), const),
                  pl.BlockSpec((1, LANES), const),
                  pl.BlockSpec((1, LANES), const),
                  pl.BlockSpec((1, d_inner), const),
                  pl.BlockSpec((1, d_inner), const)],
        out_specs=pl.BlockSpec((q, d_inner), lambda b, c: (b * nc + c, 0)),
        scratch_shapes=[pltpu.VMEM((q + 8, conv_dim), F32),
                        pltpu.VMEM((q, conv_dim), F32),
                        pltpu.VMEM((q, d_inner), F32),
                        pltpu.VMEM((n_heads // 2, SSM_STATE, LANES), F32)],
        compiler_params=_cparams(("parallel", "arbitrary")),
        name="ssd_branch",
    )(zx, dtk, conv_w.astype(F32), conv_b.astype(F32).reshape(1, conv_dim), dtb, aneg, dskip,
      gn_m.astype(F32).reshape(1, d_inner))


def _select_kernel(qi_ref, w_ref, kidx_ref, bias_ref, keys_ref, *, tq, tk, topk, idx_scale):
    seq = kidx_ref.shape[0]
    j = pl.program_id(1)
    n_tiles = ((j + 1) * tq + tk - 1) // tk
    lane = lax.broadcasted_iota(I32, (1, tq), 1)
    limit = ((j * tq + lane) // CHUNK + 1) * CHUNK
    row_local = lax.broadcasted_iota(I32, (tk, tq), 0)

    def score_tile(kt, carry):
        r0 = pl.multiple_of(kt * tk, tk)
        kb = kidx_ref[pl.ds(r0, tk), :]
        acc = jnp.zeros((tk, tq), F32)
        for h in range(IDX_HEADS):
            s = jnp.dot(kb, qi_ref[h * IDX_HEAD_DIM:(h + 1) * IDX_HEAD_DIM, :],
                        preferred_element_type=F32)
            acc = acc + jnp.maximum(s, 0.0) * w_ref[h:h + 1, :]
        sc = acc * idx_scale + 0.0
        bits = lax.bitcast_convert_type(sc, I32)
        key = jnp.where(bits < 0, bits ^ 0x7FFFFFFF, bits)
        key = jnp.where(row_local + r0 < limit, key, INT_MIN)
        keys_ref[pl.ds(r0, tk), :] = key
        return carry

    lax.fori_loop(0, n_tiles, score_tile, 0)

    def count(pred):
        def body(kt, acc):
            r0 = pl.multiple_of(kt * tk, tk)
            ones = jnp.where(pred(keys_ref[pl.ds(r0, tk), :], r0), 1, 0).astype(I32)
            return acc + jnp.sum(ones.reshape(tk // 8, 8, tq), axis=0)
        acc = lax.fori_loop(0, n_tiles, body, jnp.zeros((8, tq), I32))
        return jnp.sum(acc, axis=0, keepdims=True)

    def bisect(i, thr):
        cand = thr + jnp.left_shift(jnp.int32(1), 31 - i)
        c = count(lambda kk, r0: kk >= cand)
        return jnp.where(c >= topk, cand, thr)

    thr = lax.fori_loop(0, 32, bisect, jnp.full((1, tq), INT_MIN, I32))

    c_ge = count(lambda kk, r0: kk >= thr)
    c_gt = count(lambda kk, r0: kk > thr)
    need = topk - c_gt
    real = thr > INT_MIN
    tie_excess = jnp.max(jnp.where(jnp.logical_and(real, c_ge > topk), 1, 0)) > 0
    idx_bits = seq.bit_length()

    def tie_limit():
        def bisect_idx(i, lim):
            cand = lim + jnp.left_shift(jnp.int32(1), idx_bits - 1 - i)
            c = count(lambda kk, r0: jnp.logical_and(kk == thr, row_local + r0 < cand))
            return jnp.where(c <= need, cand, lim)
        return lax.fori_loop(0, idx_bits, bisect_idx, jnp.zeros((1, tq), I32))

    lim = lax.cond(tie_excess, tie_limit, lambda: jnp.full((1, tq), 2 ** idx_bits - 1, I32))
    lim = jnp.where(real, lim, 0)

    def write_tile(kt, carry):
        r0 = pl.multiple_of(kt * tk, tk)
        kk = keys_ref[pl.ds(r0, tk), :]
        sel = jnp.logical_or(kk > thr, jnp.logical_and(kk == thr, row_local + r0 < lim))
        bias_ref[pl.ds(r0, tk), :] = jnp.where(sel, 0.0, NEG_BIG).astype(bias_ref.dtype)
        return carry

    lax.fori_loop(0, n_tiles, write_tile, 0)

    def fill_tile(kt, carry):
        r0 = pl.multiple_of(kt * tk, tk)
        bias_ref[pl.ds(r0, tk), :] = jnp.full((tk, tq), NEG_BIG, bias_ref.dtype)
        return carry

    lax.fori_loop(n_tiles, seq // tk, fill_tile, 0)


def _select_bias(qi_t, w_t, kidx, bsz, seq, tq, tk):
    nq = seq // tq
    topk = min(TOPK_MAX, seq // 4)
    idx_scale = (IDX_HEAD_DIM ** -0.5) * (IDX_HEADS ** -0.5)
    n_qi = qi_t.shape[1]
    return pl.pallas_call(
        functools.partial(_select_kernel, tq=tq, tk=tk, topk=topk, idx_scale=idx_scale),
        out_shape=jax.ShapeDtypeStruct((bsz, seq, seq), BF16),
        grid=(bsz, nq),
        in_specs=[pl.BlockSpec((None, n_qi, tq), lambda b, j: (b * nq + j, 0, 0)),
                  pl.BlockSpec((None, IDX_HEADS, tq), lambda b, j: (b * nq + j, 0, 0)),
                  pl.BlockSpec((None, seq, IDX_HEAD_DIM), lambda b, j: (b, 0, 0))],
        out_specs=pl.BlockSpec((None, seq, tq), lambda b, j: (b, 0, j)),
        scratch_shapes=[pltpu.VMEM((seq, tq), I32)],
        compiler_params=_cparams(("parallel", "arbitrary")),
        name="topk_select",
    )(qi_t, w_t, kidx)


def _grouped_loop(n, group, body, carry):
    carry = lax.fori_loop(0, n // group, lambda g, cr: body(g * group, group, cr), carry)
    done = (n // group) * group
    size = group // 2
    while size >= 1:
        take = (n // size) % 2
        carry = lax.fori_loop(0, take, lambda _, cr, done=done, size=size: body(done, size, cr), carry)
        done = done + take * size
        size //= 2
    return carry


def _attn_kernel(q_ref, k_ref, v_ref, *rest, tq, tk, heads, n_seg, group, c):
    bias_refs = rest[:n_seg]
    gate_ref, o_ref, s_ref = rest[n_seg:]
    hd = ATTN_HEAD_DIM
    n_tiles = pl.program_id(2) + 1
    tps = bias_refs[0].shape[0] // tk
    qs = [q_ref[i * hd:(i + 1) * hd, :] for i in range(heads)]

    m8 = tuple(jnp.full((8, tq), NEG_BIG, F32) for _ in range(heads))
    for r in range(n_seg):
        def pass1(t0, cnt, m8s, r=r):
            rows = cnt * tk
            r0 = pl.multiple_of((r * tps + t0) * tk, tk)
            b = bias_refs[r][pl.ds(pl.multiple_of(t0 * tk, tk), rows), :].astype(F32)
            out = []
            for i in range(heads):
                kb = k_ref[pl.ds(r0, rows), i * hd:(i + 1) * hd]
                s = jnp.dot(kb, qs[i], preferred_element_type=F32) * c + b
                s_ref[i, pl.ds(r0, rows), :] = s
                out.append(jnp.maximum(m8s[i], jnp.max(s.reshape(rows // 8, 8, tq), axis=0)))
            return tuple(out)

        m8 = _grouped_loop(jnp.clip(n_tiles - r * tps, 0, tps), group, pass1, m8)
    ms = [jnp.max(m, axis=0, keepdims=True) for m in m8]

    ones = jnp.ones((16, tk), BF16)

    def pass2(t0, cnt, accs):
        r0 = pl.multiple_of(t0 * tk, tk)
        out = []
        for i in range(heads):
            p = jnp.exp2(s_ref[i, pl.ds(r0, cnt * tk), :] - ms[i]).astype(BF16)
            acc = accs[i]
            for u in range(cnt):
                lhs = jnp.concatenate([v_ref[t0 + u, i * hd:(i + 1) * hd, :], ones], axis=0)
                acc = acc + jnp.dot(lhs, p[u * tk:(u + 1) * tk], preferred_element_type=F32)
            out.append(acc)
        return tuple(out)

    accs = _grouped_loop(n_tiles, group, pass2,
                         tuple(jnp.zeros((hd + 16, tq), F32) for _ in range(heads)))
    o = jnp.concatenate([a[:hd] / a[hd:hd + 1] for a in accs], axis=0).T
    o_ref[...] = (o * _silu(gate_ref[...])).astype(o_ref.dtype)


def _attention(q_t, k, v_t, bias, gates, bsz, seq, tq, heads=2, n_seg=2, group=8):
    tk = tq
    nq = seq // tq
    nkt = seq // tk
    d_attn = k.shape[2]
    hw = heads * ATTN_HEAD_DIM
    c = (ATTN_HEAD_DIM ** -0.5) * 1.4426950408889634
    seg = seq // n_seg
    tps = seg // tk
    bias_specs = [pl.BlockSpec((None, seg, tq), lambda b, h, j, r=r: (b, r, jnp.maximum(j, r * tps)))
                  for r in range(n_seg)]
    return pl.pallas_call(
        functools.partial(_attn_kernel, tq=tq, tk=tk, heads=heads, n_seg=n_seg, group=group, c=c),
        out_shape=jax.ShapeDtypeStruct((bsz * seq, d_attn), BF16),
        grid=(bsz, d_attn // hw, nq),
        in_specs=[pl.BlockSpec((None, hw, tq), lambda b, h, j: (b * nq + j, h, 0)),
                  pl.BlockSpec((None, seq, hw), lambda b, h, j: (b, 0, h)),
                  pl.BlockSpec((nkt, hw, tk), lambda b, h, j: (b, h, 0))]
                 + bias_specs
                 + [pl.BlockSpec((tq, hw), lambda b, h, j: (b * nq + j, h))],
        out_specs=pl.BlockSpec((tq, hw), lambda b, h, j: (b * nq + j, h)),
        scratch_shapes=[pltpu.VMEM((heads, seq, tq), F32)],
        compiler_params=_cparams(("parallel", "parallel", "arbitrary")),
        name="masked_attention",
    )(q_t, k, v_t, *([bias] * n_seg), gates)


def _merge_kernel(ym_ref, ya_ref, wm_ref, wa_ref, gm_ref, ga_ref, bm_ref, ba_ref, o_ref):
    o_m = jnp.dot(ym_ref[...], wm_ref[...], preferred_element_type=F32)
    o_a = jnp.dot(ya_ref[...], wa_ref[...], preferred_element_type=F32)
    g_m = jax.nn.sigmoid(gm_ref[...] + bm_ref[...])
    g_a = jax.nn.sigmoid(ga_ref[...] + ba_ref[...])
    o_ref[...] = (g_m * o_m + g_a * o_a).astype(o_ref.dtype)


def _merge(y_m, y_a, w_m, w_a, gates, gate_bias, d_attn, tm=1024, tn=512):
    t, d_inner = y_m.shape
    d_model = w_m.shape[1]
    nb = d_model // tn
    off_m = d_attn // tn
    off_a = off_m + nb
    gb = gate_bias.astype(F32).reshape(1, N_BRANCHES * d_model)
    return pl.pallas_call(
        _merge_kernel,
        out_shape=jax.ShapeDtypeStruct((t, d_model), BF16),
        grid=(t // tm, nb),
        in_specs=[pl.BlockSpec((tm, d_inner), lambda i, j: (i, 0)),
                  pl.BlockSpec((tm, d_attn), lambda i, j: (i, 0)),
                  pl.BlockSpec((d_inner, tn), lambda i, j: (0, j)),
                  pl.BlockSpec((d_attn, tn), lambda i, j: (0, j)),
                  pl.BlockSpec((tm, tn), lambda i, j: (i, off_m + j)),
                  pl.BlockSpec((tm, tn), lambda i, j: (i, off_a + j)),
                  pl.BlockSpec((1, tn), lambda i, j: (0, j)),
                  pl.BlockSpec((1, tn), lambda i, j: (0, nb + j))],
        out_specs=pl.BlockSpec((tm, tn), lambda i, j: (i, j)),
        compiler_params=_cparams(("parallel", "arbitrary")),
        name="branch_merge",
    )(y_m, y_a, w_m, w_a, gates, gates, gb, gb)


def _out_kernel(m_ref, w_ref, x_ref, o_ref):
    o_ref[...] = x_ref[...] + jnp.dot(m_ref[...], w_ref[...], preferred_element_type=F32)


def _out_norm_kernel(m_ref, w_ref, x_ref, g_ref, o_ref):
    y = x_ref[...] + jnp.dot(m_ref[...], w_ref[...], preferred_element_type=F32)
    ms = jnp.mean(y * y, axis=-1, keepdims=True)
    o_ref[...] = y * lax.rsqrt(ms + EPS) * g_ref[...]


def _out_proj(merged, w_out, x2d, norm_g, tm=512):
    t, d = x2d.shape
    row = pl.BlockSpec((tm, d), lambda i: (i, 0))
    in_specs = [row, pl.BlockSpec((d, d), lambda i: (0, 0)), row]
    args = [merged, w_out, x2d]
    kern = _out_kernel
    if norm_g is not None:
        in_specs.append(pl.BlockSpec((1, d), lambda i: (0, 0)))
        args.append(norm_g.astype(F32).reshape(1, d))
        kern = _out_norm_kernel
    return pl.pallas_call(
        kern,
        out_shape=jax.ShapeDtypeStruct((t, d), F32),
        grid=(t // tm,),
        in_specs=in_specs,
        out_specs=row,
        compiler_params=_cparams(("parallel",)),
        name="out_proj",
    )(*args)


def kernel(x, w_in, conv_w, conv_b, dt_bias, a_log, d_skip, gn_m, gate_bias, w_branch, w_out,
           norm_in, norm_final):
    bsz, seq, d_model = x.shape
    depth = w_in.shape[0]
    d_inner = gn_m.shape[1]
    conv_dim = conv_w.shape[2]
    n_heads = dt_bias.shape[1]
    d_attn = w_branch.shape[1] - d_inner
    n_qi = IDX_HEADS * IDX_HEAD_DIM
    t = bsz * seq
    tq_sel, tk_sel = 128, 512
    tq_att = 256
    tm = 1024

    o_z = 0
    o_xbc = o_z + d_inner
    o_dt = o_xbc + conv_dim
    o_q = o_dt + n_heads
    o_k = o_q + d_attn
    o_v = o_k + d_attn
    o_ga = o_v + d_attn
    o_qi = o_ga + d_attn
    o_ki = o_qi + n_qi
    o_wi = o_ki + IDX_HEAD_DIM
    o_g = o_wi + IDX_HEADS
    small_pad = LANES - n_heads - IDX_HEAD_DIM

    x2d = x.reshape(t, d_model)
    for i in range(depth):
        w = w_in[i].astype(BF16)
        w_zx = w[:, o_z:o_dt]
        w_small = jnp.concatenate([w[:, o_dt:o_q], w[:, o_ki:o_wi],
                                   jnp.zeros((d_model, small_pad), BF16)], axis=1)
        w_k = w[:, o_k:o_v]
        w_gates = jnp.concatenate([w[:, o_ga:o_qi], w[:, o_g:]], axis=1)
        wt_q = w[:, o_q:o_k].T
        wt_v = w[:, o_v:o_ga].T
        wt_qi = w[:, o_qi:o_ki].T
        wt_wi = w[:, o_wi:o_g].T

        h = _rmsnorm(x2d, norm_in[i].astype(F32), BF16)
        zx = _matmul_nn(h, w_zx, F32, tm, 512, "proj_zx")
        small = _matmul_nn(h, w_small, F32, tm, LANES, "proj_small")
        k_tok = _matmul_nn(h, w_k, BF16, tm, 512, "proj_k")
        gates = _matmul_nn(h, w_gates, F32, tm, 512, "proj_gates")
        q_t = _matmul_nt(wt_q, h, BF16, tm, 512, tq_att, "proj_qT")
        v_t = _matmul_nt(wt_v, h, BF16, tm, 512, tq_att, "proj_vT")
        qi_t = _matmul_nt(wt_qi, h, BF16, tm, 512, tq_sel, "proj_qidxT")
        wi_t = _matmul_nt(wt_wi, h, F32, tm, IDX_HEADS, tq_sel, "proj_widxT")

        y_m = _ssd_branch(zx, small, conv_w[i], conv_b[i], dt_bias[i], a_log[i], d_skip[i], gn_m[i],
                          bsz, seq)

        kidx = small[:, n_heads:n_heads + IDX_HEAD_DIM].astype(BF16).reshape(bsz, seq, IDX_HEAD_DIM)
        bias = _select_bias(qi_t, wi_t, kidx, bsz, seq, tq_sel, tk_sel)
        y_a = _attention(q_t, k_tok.reshape(bsz, seq, d_attn), v_t, bias, gates, bsz, seq, tq_att)

        wb = w_branch[i].astype(BF16)
        merged = _merge(y_m, y_a, wb[:d_inner], wb[d_inner:], gates, gate_bias[i], d_attn)
        last = i == depth - 1
        x2d = _out_proj(merged, w_out[i].astype(BF16), x2d, norm_final if last else None)
    return x2d.reshape(bsz, seq, d_model)
```

```python
import functools

import jax
import jax.numpy as jnp
from jax import lax
from jax.experimental import pallas as pl
from jax.experimental.pallas import tpu as pltpu

F32 = jnp.float32
BF16 = jnp.bfloat16
I32 = jnp.int32

EPS = 1e-6
CHUNK = 64
SSM_HEAD_DIM = 64
SSM_GROUPS = 4
SSM_STATE = 128
SSM_CONV = 4
ATTN_HEAD_DIM = 128
IDX_HEADS = 16
IDX_HEAD_DIM = 64
TOPK_MAX = 256
N_BRANCHES = 2

LANES = 128
NEG_BIG = -1e30
INT_MIN = -2 ** 31

VMEM_LIMIT = 56 * 1024 * 1024


def _cparams(sem):
    return pltpu.CompilerParams(dimension_semantics=sem, vmem_limit_bytes=VMEM_LIMIT)


def _rmsnorm_kernel(x_ref, g_ref, o_ref):
    x = x_ref[...]
    ms = jnp.mean(x * x, axis=-1, keepdims=True)
    o_ref[...] = (x * lax.rsqrt(ms + EPS) * g_ref[...]).astype(o_ref.dtype)


def _rmsnorm(x2d, g, out_dtype, tm=512):
    t, d = x2d.shape
    return pl.pallas_call(
        _rmsnorm_kernel,
        out_shape=jax.ShapeDtypeStruct((t, d), out_dtype),
        grid=(t // tm,),
        in_specs=[pl.BlockSpec((tm, d), lambda i: (i, 0)),
                  pl.BlockSpec((1, d), lambda i: (0, 0))],
        out_specs=pl.BlockSpec((tm, d), lambda i: (i, 0)),
        compiler_params=_cparams(("parallel",)),
        name="rmsnorm",
    )(x2d, g.reshape(1, d))


def _matmul_nn_kernel(a_ref, b_ref, o_ref):
    o_ref[...] = jnp.dot(a_ref[...], b_ref[...], preferred_element_type=F32).astype(o_ref.dtype)


def _matmul_nn(a, b, out_dtype, tm, tn, name):
    t, k = a.shape
    n = b.shape[1]
    return pl.pallas_call(
        _matmul_nn_kernel,
        out_shape=jax.ShapeDtypeStruct((t, n), out_dtype),
        grid=(t // tm, n // tn),
        in_specs=[pl.BlockSpec((tm, k), lambda i, j: (i, 0)),
                  pl.BlockSpec((k, tn), lambda i, j: (0, j))],
        out_specs=pl.BlockSpec((tm, tn), lambda i, j: (i, j)),
        compiler_params=_cparams(("parallel", "arbitrary")),
        name=name,
    )(a, b)


def _matmul_nt_kernel(w_ref, a_ref, o_ref, *, tt):
    res = lax.dot_general(w_ref[...], a_ref[...], (((1,), (1,)), ((), ())),
                          preferred_element_type=F32).astype(o_ref.dtype)
    for c in range(o_ref.shape[0]):
        o_ref[c] = res[:, c * tt:(c + 1) * tt]


def _matmul_nt(wt, a, out_dtype, tm, tn, tt, name):
    t, k = a.shape
    n = wt.shape[0]
    return pl.pallas_call(
        functools.partial(_matmul_nt_kernel, tt=tt),
        out_shape=jax.ShapeDtypeStruct((t // tt, n, tt), out_dtype),
        grid=(t // tm, n // tn),
        in_specs=[pl.BlockSpec((tn, k), lambda i, j: (j, 0)),
                  pl.BlockSpec((tm, k), lambda i, j: (i, 0))],
        out_specs=pl.BlockSpec((tm // tt, tn, tt), lambda i, j: (i, j, 0)),
        compiler_params=_cparams(("parallel", "arbitrary")),
        name=name,
    )(wt, a)


def _softplus(x):
    return jnp.maximum(x, 0.0) + jnp.log1p(jnp.exp(-jnp.abs(x)))


def _silu(x):
    return x * jax.nn.sigmoid(x)


def _ssd_kernel(zx_ref, dt_ref, convw_ref, convb_ref, dtb_ref, aneg_ref, dskip_ref, gnm_ref,
                o_ref, ext_ref, xs_ref, y_ref, state_ref, *, q, d_inner, conv_dim):
    n_heads = d_inner // SSM_HEAD_DIM
    n_pairs = n_heads // 2
    pairs_per_group = n_pairs // SSM_GROUPS
    halo = 8

    @pl.when(pl.program_id(1) == 0)
    def _():
        ext_ref[0:halo, :] = jnp.zeros((halo, conv_dim), F32)
        state_ref[...] = jnp.zeros(state_ref.shape, F32)

    ext_ref[halo:halo + q, :] = zx_ref[:, d_inner:d_inner + conv_dim]
    ct = 512
    for c0 in range(0, conv_dim, ct):
        base = halo - (SSM_CONV - 1)
        acc = ext_ref[base:base + q, c0:c0 + ct] * convw_ref[0:1, c0:c0 + ct]
        for kk in range(1, SSM_CONV):
            acc = acc + ext_ref[base + kk:base + kk + q, c0:c0 + ct] * convw_ref[kk:kk + 1, c0:c0 + ct]
        xs_ref[:, c0:c0 + ct] = _silu(acc + convb_ref[:, c0:c0 + ct])
    ext_ref[0:halo, :] = ext_ref[q:q + halo, :]

    dt = _softplus(dt_ref[...] + dtb_ref[...])
    a = dt * aneg_ref[...]
    row_i = lax.broadcasted_iota(I32, (q, q), 0)
    col_i = lax.broadcasted_iota(I32, (q, q), 1)
    tril = row_i >= col_i
    acum = jnp.dot(tril.astype(F32), a, precision=lax.Precision.HIGHEST,
                   preferred_element_type=F32)
    acum_t = acum.T
    a_last = acum[q - 1:q, :]
    e_in = jnp.exp(acum)
    e_out = jnp.exp(a_last - acum)
    e_last = jnp.exp(a_last)

    lane_lo = lax.broadcasted_iota(I32, (q, LANES), 1) < SSM_HEAD_DIM
    lane_lo_row = lax.broadcasted_iota(I32, (1, LANES), 1) < SSM_HEAD_DIM
    bc_off = d_inner
    cc_off = d_inner + SSM_GROUPS * SSM_STATE

    for g in range(SSM_GROUPS):
        bg = xs_ref[:, bc_off + g * SSM_STATE:bc_off + (g + 1) * SSM_STATE]
        cg = xs_ref[:, cc_off + g * SSM_STATE:cc_off + (g + 1) * SSM_STATE]
        bg16 = bg.astype(BF16)
        cg16 = cg.astype(BF16)
        cb = lax.dot_general(cg16, bg16, (((1,), (1,)), ((), ())), preferred_element_type=F32)
        bgt16 = bg.T.astype(BF16)
        for pi in range(pairs_per_group):
            pp = g * pairs_per_group + pi
            h0, h1 = 2 * pp, 2 * pp + 1
            x_pair = xs_ref[:, pp * LANES:(pp + 1) * LANES]
            xdt = x_pair * jnp.where(lane_lo, dt[:, h0:h0 + 1], dt[:, h1:h1 + 1])
            y_pair = dskip_ref[:, pp * LANES:(pp + 1) * LANES] * x_pair
            for hh, keep in ((h0, lane_lo), (h1, jnp.logical_not(lane_lo))):
                diff = acum[:, hh:hh + 1] - acum_t[hh:hh + 1, :]
                dec = jnp.exp(jnp.where(tril, diff, NEG_BIG))
                m16 = (cb * dec).astype(BF16)
                xm16 = jnp.where(keep, xdt, 0.0).astype(BF16)
                y_pair = y_pair + jnp.dot(m16, xm16, preferred_element_type=F32)
            st = state_ref[pp]
            e_in_pair = jnp.where(lane_lo, e_in[:, h0:h0 + 1], e_in[:, h1:h1 + 1])
            y_pair = y_pair + jnp.dot(cg16, st.astype(BF16), preferred_element_type=F32) * e_in_pair
            e_out_pair = jnp.where(lane_lo, e_out[:, h0:h0 + 1], e_out[:, h1:h1 + 1])
            wx16 = (xdt * e_out_pair).astype(BF16)
            e_last_pair = jnp.where(lane_lo_row, e_last[:, h0:h0 + 1], e_last[:, h1:h1 + 1])
            state_ref[pp] = st * e_last_pair + jnp.dot(bgt16, wx16, preferred_element_type=F32)
            y_ref[:, pp * LANES:(pp + 1) * LANES] = y_pair

    gw = d_inner // SSM_GROUPS
    for g in range(SSM_GROUPS):
        z = zx_ref[:, g * gw:(g + 1) * gw]
        yg = y_ref[:, g * gw:(g + 1) * gw] * _silu(z)
        ms = jnp.mean(yg * yg, axis=-1, keepdims=True)
        o_ref[:, g * gw:(g + 1) * gw] = (yg * lax.rsqrt(ms + EPS) * gnm_ref[:, g * gw:(g + 1) * gw]).astype(o_ref.dtype)


def _ssd_branch(zx, dtk, conv_w, conv_b, dt_bias, a_log, d_skip, gn_m, bsz, seq, q=256):
    t = zx.shape[0]
    d_inner = gn_m.shape[0]
    conv_dim = conv_w.shape[1]
    n_heads = d_inner // SSM_HEAD_DIM
    nc = seq // q
    pad = LANES - n_heads
    dtb = jnp.pad(dt_bias.astype(F32), (0, pad)).reshape(1, LANES)
    aneg = jnp.pad(-jnp.exp(a_log.astype(F32)), (0, pad)).reshape(1, LANES)
    dskip = jnp.repeat(d_skip.astype(F32), SSM_HEAD_DIM).reshape(1, d_inner)
    const = lambda b, c: (0, 0)
    return pl.pallas_call(
        functools.partial(_ssd_kernel, q=q, d_inner=d_inner, conv_dim=conv_dim),
        out_shape=jax.ShapeDtypeStruct((t, d_inner), BF16),
        grid=(bsz, nc),
        in_specs=[pl.BlockSpec((q, d_inner + conv_dim), lambda b, c: (b * nc + c, 0)),
                  pl.BlockSpec((q, LANES), lambda b, c: (b * nc + c, 0)),
                  pl.BlockSpec((SSM_CONV, conv_dim), const),
                  pl.BlockSpec((1, conv_dim), const),
                  pl.BlockSpec((1, LANES), const),
                  pl.BlockSpec((1, LANES), const),
                  pl.BlockSpec((1, d_inner), const),
                  pl.BlockSpec((1, d_inner), const)],
        out_specs=pl.BlockSpec((q, d_inner), lambda b, c: (b * nc + c, 0)),
        scratch_shapes=[pltpu.VMEM((q + 8, conv_dim), F32),
                        pltpu.VMEM((q, conv_dim), F32),
                        pltpu.VMEM((q, d_inner), F32),
                        pltpu.VMEM((n_heads // 2, SSM_STATE, LANES), F32)],
        compiler_params=_cparams(("parallel", "arbitrary")),
        name="ssd_branch",
    )(zx, dtk, conv_w.astype(F32), conv_b.astype(F32).reshape(1, conv_dim), dtb, aneg, dskip,
      gn_m.astype(F32).reshape(1, d_inner))


I16 = jnp.int16
I16_MIN = -2 ** 15


def _select_kernel(qi_ref, w_ref, kidx_ref, bias_ref, keys_ref, hi_ref, lo_ref, *, tq, tk, topk, idx_scale):
    seq = kidx_ref.shape[0]
    j = pl.program_id(1)
    n_tiles = ((j + 1) * tq + tk - 1) // tk
    lane = lax.broadcasted_iota(I32, (1, tq), 1)
    limit = ((j * tq + lane) // CHUNK + 1) * CHUNK
    row_local = lax.broadcasted_iota(I32, (tk, tq), 0)

    def score_tile(kt, carry):
        r0 = pl.multiple_of(kt * tk, tk)
        kb = kidx_ref[pl.ds(r0, tk), :]
        acc = jnp.zeros((tk, tq), F32)
        for h in range(IDX_HEADS):
            s = jnp.dot(kb, qi_ref[h * IDX_HEAD_DIM:(h + 1) * IDX_HEAD_DIM, :],
                        preferred_element_type=F32)
            acc = acc + jnp.maximum(s, 0.0) * w_ref[h:h + 1, :]
        sc = acc * idx_scale + 0.0
        bits = lax.bitcast_convert_type(sc, I32)
        key = jnp.where(bits < 0, bits ^ 0x7FFFFFFF, bits)
        key = jnp.where(row_local + r0 < limit, key, INT_MIN)
        keys_ref[pl.ds(r0, tk), :] = key
        hi_ref[pl.ds(r0, tk), :] = (key >> 16).astype(I16)
        return carry

    lax.fori_loop(0, n_tiles, score_tile, 0)

    def count(pred):
        def body(kt, acc):
            r0 = pl.multiple_of(kt * tk, tk)
            ones = jnp.where(pred(keys_ref[pl.ds(r0, tk), :], r0), 1, 0).astype(I32)
            return acc + jnp.sum(ones.reshape(tk // 8, 8, tq), axis=0)
        acc = lax.fori_loop(0, n_tiles, body, jnp.zeros((8, tq), I32))
        return jnp.sum(acc, axis=0, keepdims=True)

    def count16(ref, cand):
        cand16 = cand.astype(I16)

        def body(kt, acc):
            r0 = pl.multiple_of(kt * tk, tk)
            ones = jnp.where(ref[pl.ds(r0, tk), :] >= cand16, jnp.int16(1), jnp.int16(0))
            parts = [ones[r * 16:(r + 1) * 16] for r in range(tk // 16)]
            while len(parts) > 1:
                parts = [a + b for a, b in zip(parts[0::2], parts[1::2])]
            return acc + parts[0]
        acc = lax.fori_loop(0, n_tiles, body, jnp.zeros((16, tq), I16))
        return jnp.sum(acc.astype(I32), axis=0, keepdims=True)

    def kth_largest16(ref, kth):
        def bisect(i, thr):
            cand = thr + jnp.left_shift(jnp.int32(1), 15 - i)
            return jnp.where(count16(ref, cand) >= kth, cand, thr)
        return lax.fori_loop(0, 16, bisect, jnp.full((1, tq), I16_MIN, I32))

    thr_hi = kth_largest16(hi_ref, topk)
    above = count16(hi_ref, thr_hi + 1)

    def low_tile(kt, carry):
        r0 = pl.multiple_of(kt * tk, tk)
        kk = keys_ref[pl.ds(r0, tk), :]
        lo = (kk & 0xFFFF) + I16_MIN
        lo_ref[pl.ds(r0, tk), :] = jnp.where((kk >> 16) == thr_hi, lo, I16_MIN).astype(I16)
        return carry

    lax.fori_loop(0, n_tiles, low_tile, 0)
    thr_lo = kth_largest16(lo_ref, topk - above)
    thr = thr_hi * 65536 + (thr_lo - I16_MIN)

    c_ge = count(lambda kk, r0: kk >= thr)
    c_gt = count(lambda kk, r0: kk > thr)
    need = topk - c_gt
    real = thr > INT_MIN
    tie_excess = jnp.max(jnp.where(jnp.logical_and(real, c_ge > topk), 1, 0)) > 0
    idx_bits = seq.bit_length()

    def tie_limit():
        def bisect_idx(i, lim):
            cand = lim + jnp.left_shift(jnp.int32(1), idx_bits - 1 - i)
            c = count(lambda kk, r0: jnp.logical_and(kk == thr, row_local + r0 < cand))
            return jnp.where(c <= need, cand, lim)
        return lax.fori_loop(0, idx_bits, bisect_idx, jnp.zeros((1, tq), I32))

    lim = lax.cond(tie_excess, tie_limit, lambda: jnp.full((1, tq), 2 ** idx_bits - 1, I32))
    lim = jnp.where(real, lim, 0)

    def write_tile(kt, carry):
        r0 = pl.multiple_of(kt * tk, tk)
        kk = keys_ref[pl.ds(r0, tk), :]
        sel = jnp.logical_or(kk > thr, jnp.logical_and(kk == thr, row_local + r0 < lim))
        bias_ref[pl.ds(r0, tk), :] = jnp.where(sel, 0.0, NEG_BIG).astype(bias_ref.dtype)
        return carry

    lax.fori_loop(0, n_tiles, write_tile, 0)

    def fill_tile(kt, carry):
        r0 = pl.multiple_of(kt * tk, tk)
        bias_ref[pl.ds(r0, tk), :] = jnp.full((tk, tq), NEG_BIG, bias_ref.dtype)
        return carry

    lax.fori_loop(n_tiles, seq // tk, fill_tile, 0)


def _select_bias(qi_t, w_t, kidx, bsz, seq, tq, tk):
    nq = seq // tq
    topk = min(TOPK_MAX, seq // 4)
    idx_scale = (IDX_HEAD_DIM ** -0.5) * (IDX_HEADS ** -0.5)
    n_qi = qi_t.shape[1]
    return pl.pallas_call(
        functools.partial(_select_kernel, tq=tq, tk=tk, topk=topk, idx_scale=idx_scale),
        out_shape=jax.ShapeDtypeStruct((bsz, seq, seq), BF16),
        grid=(bsz, nq),
        in_specs=[pl.BlockSpec((None, n_qi, tq), lambda b, j: (b * nq + j, 0, 0)),
                  pl.BlockSpec((None, IDX_HEADS, tq), lambda b, j: (b * nq + j, 0, 0)),
                  pl.BlockSpec((None, seq, IDX_HEAD_DIM), lambda b, j: (b, 0, 0))],
        out_specs=pl.BlockSpec((None, seq, tq), lambda b, j: (b, 0, j)),
        scratch_shapes=[pltpu.VMEM((seq, tq), I32), pltpu.VMEM((seq, tq), I16), pltpu.VMEM((seq, tq), I16)],
        compiler_params=_cparams(("parallel", "arbitrary")),
        name="topk_select",
    )(qi_t, w_t, kidx)


def _grouped_loop(n, group, body, carry):
    carry = lax.fori_loop(0, n // group, lambda g, cr: body(g * group, group, cr), carry)
    done = (n // group) * group
    size = group // 2
    while size >= 1:
        take = (n // size) % 2
        carry = lax.fori_loop(0, take, lambda _, cr, done=done, size=size: body(done, size, cr), carry)
        done = done + take * size
        size //= 2
    return carry


def _attn_kernel(q_ref, k_ref, v_ref, *rest, tq, tk, heads, n_seg, group, c):
    bias_refs = rest[:n_seg]
    gate_ref, o_ref, s_ref = rest[n_seg:]
    hd = ATTN_HEAD_DIM
    n_tiles = pl.program_id(2) + 1
    tps = bias_refs[0].shape[0] // tk
    qs = [q_ref[i * hd:(i + 1) * hd, :] for i in range(heads)]

    m8 = tuple(jnp.full((8, tq), NEG_BIG, F32) for _ in range(heads))
    for r in range(n_seg):
        def pass1(t0, cnt, m8s, r=r):
            rows = cnt * tk
            r0 = pl.multiple_of((r * tps + t0) * tk, tk)
            b = bias_refs[r][pl.ds(pl.multiple_of(t0 * tk, tk), rows), :].astype(F32)
            out = []
            for i in range(heads):
                kb = k_ref[pl.ds(r0, rows), i * hd:(i + 1) * hd]
                s = jnp.dot(kb, qs[i], preferred_element_type=F32) * c + b
                s_ref[i, pl.ds(r0, rows), :] = s
                out.append(jnp.maximum(m8s[i], jnp.max(s.reshape(rows // 8, 8, tq), axis=0)))
            return tuple(out)

        m8 = _grouped_loop(jnp.clip(n_tiles - r * tps, 0, tps), group, pass1, m8)
    ms = [jnp.max(m, axis=0, keepdims=True) for m in m8]

    def pass2(t0, cnt, accs):
        r0 = pl.multiple_of(t0 * tk, tk)
        ones = jnp.ones((16, cnt * tk), BF16)
        out = []
        for i in range(heads):
            p = jnp.exp2(s_ref[i, pl.ds(r0, cnt * tk), :] - ms[i]).astype(BF16)
            vt = jnp.concatenate([v_ref[t0 + u, i * hd:(i + 1) * hd, :] for u in range(cnt)], axis=1)
            lhs = jnp.concatenate([vt, ones], axis=0)
            out.append(accs[i] + jnp.dot(lhs, p, preferred_element_type=F32))
        return tuple(out)

    accs = _grouped_loop(n_tiles, group, pass2,
                         tuple(jnp.zeros((hd + 16, tq), F32) for _ in range(heads)))
    o = jnp.concatenate([a[:hd] / a[hd:hd + 1] for a in accs], axis=0).T
    o_ref[...] = (o * _silu(gate_ref[...])).astype(o_ref.dtype)


def _attention(q_t, k, v_t, bias, gates, bsz, seq, tq, heads=2, n_seg=2, group=8):
    tk = tq
    nq = seq // tq
    nkt = seq // tk
    d_attn = k.shape[2]
    hw = heads * ATTN_HEAD_DIM
    c = (ATTN_HEAD_DIM ** -0.5) * 1.4426950408889634
    seg = seq // n_seg
    tps = seg // tk
    bias_specs = [pl.BlockSpec((None, seg, tq), lambda b, h, j, r=r: (b, r, jnp.maximum(j, r * tps)))
                  for r in range(n_seg)]
    return pl.pallas_call(
        functools.partial(_attn_kernel, tq=tq, tk=tk, heads=heads, n_seg=n_seg, group=group, c=c),
        out_shape=jax.ShapeDtypeStruct((bsz * seq, d_attn), BF16),
        grid=(bsz, d_attn // hw, nq),
        in_specs=[pl.BlockSpec((None, hw, tq), lambda b, h, j: (b * nq + j, h, 0)),
                  pl.BlockSpec((None, seq, hw), lambda b, h, j: (b, 0, h)),
                  pl.BlockSpec((nkt, hw, tk), lambda b, h, j: (b, h, 0))]
                 + bias_specs
                 + [pl.BlockSpec((tq, hw), lambda b, h, j: (b * nq + j, h))],
        out_specs=pl.BlockSpec((tq, hw), lambda b, h, j: (b * nq + j, h)),
        scratch_shapes=[pltpu.VMEM((heads, seq, tq), F32)],
        compiler_params=_cparams(("parallel", "parallel", "arbitrary")),
        name="masked_attention",
    )(q_t, k, v_t, *([bias] * n_seg), gates)


def _merge_kernel(ym_ref, ya_ref, wm_ref, wa_ref, gm_ref, ga_ref, bm_ref, ba_ref, o_ref):
    o_m = jnp.dot(ym_ref[...], wm_ref[...], preferred_element_type=F32)
    o_a = jnp.dot(ya_ref[...], wa_ref[...], preferred_element_type=F32)
    g_m = jax.nn.sigmoid(gm_ref[...] + bm_ref[...])
    g_a = jax.nn.sigmoid(ga_ref[...] + ba_ref[...])
    o_ref[...] = (g_m * o_m + g_a * o_a).astype(o_ref.dtype)


def _merge(y_m, y_a, w_m, w_a, gates, gate_bias, d_attn, tm=1024, tn=512):
    t, d_inner = y_m.shape
    d_model = w_m.shape[1]
    nb = d_model // tn
    off_m = d_attn // tn
    off_a = off_m + nb
    gb = gate_bias.astype(F32).reshape(1, N_BRANCHES * d_model)
    return pl.pallas_call(
        _merge_kernel,
        out_shape=jax.ShapeDtypeStruct((t, d_model), BF16),
        grid=(t // tm, nb),
        in_specs=[pl.BlockSpec((tm, d_inner), lambda i, j: (i, 0)),
                  pl.BlockSpec((tm, d_attn), lambda i, j: (i, 0)),
                  pl.BlockSpec((d_inner, tn), lambda i, j: (0, j)),
                  pl.BlockSpec((d_attn, tn), lambda i, j: (0, j)),
                  pl.BlockSpec((tm, tn), lambda i, j: (i, off_m + j)),
                  pl.BlockSpec((tm, tn), lambda i, j: (i, off_a + j)),
                  pl.BlockSpec((1, tn), lambda i, j: (0, j)),
                  pl.BlockSpec((1, tn), lambda i, j: (0, nb + j))],
        out_specs=pl.BlockSpec((tm, tn), lambda i, j: (i, j)),
        compiler_params=_cparams(("parallel", "arbitrary")),
        name="branch_merge",
    )(y_m, y_a, w_m, w_a, gates, gates, gb, gb)


def _out_kernel(m_ref, w_ref, x_ref, o_ref):
    o_ref[...] = x_ref[...] + jnp.dot(m_ref[...], w_ref[...], preferred_element_type=F32)


def _out_norm_kernel(m_ref, w_ref, x_ref, g_ref, o_ref):
    y = x_ref[...] + jnp.dot(m_ref[...], w_ref[...], preferred_element_type=F32)
    ms = jnp.mean(y * y, axis=-1, keepdims=True)
    o_ref[...] = y * lax.rsqrt(ms + EPS) * g_ref[...]


def _out_proj(merged, w_out, x2d, norm_g, tm=512):
    t, d = x2d.shape
    row = pl.BlockSpec((tm, d), lambda i: (i, 0))
    in_specs = [row, pl.BlockSpec((d, d), lambda i: (0, 0)), row]
    args = [merged, w_out, x2d]
    kern = _out_kernel
    if norm_g is not None:
        in_specs.append(pl.BlockSpec((1, d), lambda i: (0, 0)))
        args.append(norm_g.astype(F32).reshape(1, d))
        kern = _out_norm_kernel
    return pl.pallas_call(
        kern,
        out_shape=jax.ShapeDtypeStruct((t, d), F32),
        grid=(t // tm,),
        in_specs=in_specs,
        out_specs=row,
        compiler_params=_cparams(("parallel",)),
        name="out_proj",
    )(*args)


def kernel(x, w_in, conv_w, conv_b, dt_bias, a_log, d_skip, gn_m, gate_bias, w_branch, w_out,
           norm_in, norm_final):
    bsz, seq, d_model = x.shape
    depth = w_in.shape[0]
    d_inner = gn_m.shape[1]
    conv_dim = conv_w.shape[2]
    n_heads = dt_bias.shape[1]
    d_attn = w_branch.shape[1] - d_inner
    n_qi = IDX_HEADS * IDX_HEAD_DIM
    t = bsz * seq
    tq_sel, tk_sel = 128, 512
    tq_att = 256
    tm = 1024

    o_z = 0
    o_xbc = o_z + d_inner
    o_dt = o_xbc + conv_dim
    o_q = o_dt + n_heads
    o_k = o_q + d_attn
    o_v = o_k + d_attn
    o_ga = o_v + d_attn
    o_qi = o_ga + d_attn
    o_ki = o_qi + n_qi
    o_wi = o_ki + IDX_HEAD_DIM
    o_g = o_wi + IDX_HEADS
    small_pad = LANES - n_heads - IDX_HEAD_DIM

    x2d = x.reshape(t, d_model)
    for i in range(depth):
        w = w_in[i].astype(BF16)
        w_zx = w[:, o_z:o_dt]
        w_small = jnp.concatenate([w[:, o_dt:o_q], w[:, o_ki:o_wi],
                                   jnp.zeros((d_model, small_pad), BF16)], axis=1)
        w_k = w[:, o_k:o_v]
        w_gates = jnp.concatenate([w[:, o_ga:o_qi], w[:, o_g:]], axis=1)
        wt_q = w[:, o_q:o_k].T
        wt_v = w[:, o_v:o_ga].T
        wt_qi = w[:, o_qi:o_ki].T
        wt_wi = w[:, o_wi:o_g].T

        h = _rmsnorm(x2d, norm_in[i].astype(F32), BF16)
        zx = _matmul_nn(h, w_zx, F32, tm, 512, "proj_zx")
        small = _matmul_nn(h, w_small, F32, tm, LANES, "proj_small")
        k_tok = _matmul_nn(h, w_k, BF16, tm, 512, "proj_k")
        gates = _matmul_nn(h, w_gates, F32, tm, 512, "proj_gates")
        q_t = _matmul_nt(wt_q, h, BF16, tm, 512, tq_att, "proj_qT")
        v_t = _matmul_nt(wt_v, h, BF16, tm, 512, tq_att, "proj_vT")
        qi_t = _matmul_nt(wt_qi, h, BF16, tm, 512, tq_sel, "proj_qidxT")
        wi_t = _matmul_nt(wt_wi, h, F32, tm, IDX_HEADS, tq_sel, "proj_widxT")

        y_m = _ssd_branch(zx, small, conv_w[i], conv_b[i], dt_bias[i], a_log[i], d_skip[i], gn_m[i],
                          bsz, seq)

        kidx = small[:, n_heads:n_heads + IDX_HEAD_DIM].astype(BF16).reshape(bsz, seq, IDX_HEAD_DIM)
        bias = _select_bias(qi_t, wi_t, kidx, bsz, seq, tq_sel, tk_sel)
        y_a = _attention(q_t, k_tok.reshape(bsz, seq, d_attn), v_t, bias, gates, bsz, seq, tq_att)

        wb = w_branch[i].astype(BF16)
        merged = _merge(y_m, y_a, wb[:d_inner], wb[d_inner:], gates, gate_bias[i], d_attn)
        last = i == depth - 1
        x2d = _out_proj(merged, w_out[i].astype(BF16), x2d, norm_final if last else None)
    return x2d.reshape(bsz, seq, d_model)
```

```python
import functools

import jax
import jax.numpy as jnp
from jax import lax
from jax.experimental import pallas as pl
from jax.experimental.pallas import tpu as pltpu

F32 = jnp.float32
BF16 = jnp.bfloat16
I32 = jnp.int32

EPS = 1e-6
CHUNK = 64
SSM_HEAD_DIM = 64
SSM_GROUPS = 4
SSM_STATE = 128
SSM_CONV = 4
ATTN_HEAD_DIM = 128
IDX_HEADS = 16
IDX_HEAD_DIM = 64
TOPK_MAX = 256
N_BRANCHES = 2

LANES = 128
NEG_BIG = -1e30
INT_MIN = -2 ** 31

VMEM_LIMIT = 56 * 1024 * 1024


def _cparams(sem):
    return pltpu.CompilerParams(dimension_semantics=sem, vmem_limit_bytes=VMEM_LIMIT)


def _rmsnorm_kernel(x_ref, g_ref, o_ref):
    x = x_ref[...]
    ms = jnp.mean(x * x, axis=-1, keepdims=True)
    o_ref[...] = (x * lax.rsqrt(ms + EPS) * g_ref[...]).astype(o_ref.dtype)


def _rmsnorm(x2d, g, out_dtype, tm=512):
    t, d = x2d.shape
    return pl.pallas_call(
        _rmsnorm_kernel,
        out_shape=jax.ShapeDtypeStruct((t, d), out_dtype),
        grid=(t // tm,),
        in_specs=[pl.BlockSpec((tm, d), lambda i: (i, 0)),
                  pl.BlockSpec((1, d), lambda i: (0, 0))],
        out_specs=pl.BlockSpec((tm, d), lambda i: (i, 0)),
        compiler_params=_cparams(("parallel",)),
        name="rmsnorm",
    )(x2d, g.reshape(1, d))


def _matmul_nn_kernel(a_ref, b_ref, o_ref):
    o_ref[...] = jnp.dot(a_ref[...], b_ref[...], preferred_element_type=F32).astype(o_ref.dtype)


def _matmul_nn(a, b, out_dtype, tm, tn, name):
    t, k = a.shape
    n = b.shape[1]
    return pl.pallas_call(
        _matmul_nn_kernel,
        out_shape=jax.ShapeDtypeStruct((t, n), out_dtype),
        grid=(t // tm, n // tn),
        in_specs=[pl.BlockSpec((tm, k), lambda i, j: (i, 0)),
                  pl.BlockSpec((k, tn), lambda i, j: (0, j))],
        out_specs=pl.BlockSpec((tm, tn), lambda i, j: (i, j)),
        compiler_params=_cparams(("parallel", "arbitrary")),
        name=name,
    )(a, b)


def _matmul_nt_kernel(w_ref, a_ref, o_ref, *, tt):
    res = lax.dot_general(w_ref[...], a_ref[...], (((1,), (1,)), ((), ())),
                          preferred_element_type=F32).astype(o_ref.dtype)
    for c in range(o_ref.shape[0]):
        o_ref[c] = res[:, c * tt:(c + 1) * tt]


def _matmul_nt(wt, a, out_dtype, tm, tn, tt, name):
    t, k = a.shape
    n = wt.shape[0]
    return pl.pallas_call(
        functools.partial(_matmul_nt_kernel, tt=tt),
        out_shape=jax.ShapeDtypeStruct((t // tt, n, tt), out_dtype),
        grid=(t // tm, n // tn),
        in_specs=[pl.BlockSpec((tn, k), lambda i, j: (j, 0)),
                  pl.BlockSpec((tm, k), lambda i, j: (i, 0))],
        out_specs=pl.BlockSpec((tm // tt, tn, tt), lambda i, j: (i, j, 0)),
        compiler_params=_cparams(("parallel", "arbitrary")),
        name=name,
    )(wt, a)


def _softplus(x):
    return jnp.maximum(x, 0.0) + jnp.log1p(jnp.exp(-jnp.abs(x)))


def _silu(x):
    return x * jax.nn.sigmoid(x)


def _ssd_kernel(zx_ref, dt_ref, convw_ref, convb_ref, dtb_ref, aneg_ref, dskip_ref, gnm_ref,
                o_ref, ext_ref, xs_ref, y_ref, state_ref, *, q, d_inner, conv_dim):
    n_heads = d_inner // SSM_HEAD_DIM
    n_pairs = n_heads // 2
    pairs_per_group = n_pairs // SSM_GROUPS
    halo = 8

    @pl.when(pl.program_id(1) == 0)
    def _():
        ext_ref[0:halo, :] = jnp.zeros((halo, conv_dim), F32)
        state_ref[...] = jnp.zeros(state_ref.shape, F32)

    ext_ref[halo:halo + q, :] = zx_ref[:, d_inner:d_inner + conv_dim]
    ct = 512
    for c0 in range(0, conv_dim, ct):
        base = halo - (SSM_CONV - 1)
        acc = ext_ref[base:base + q, c0:c0 + ct] * convw_ref[0:1, c0:c0 + ct]
        for kk in range(1, SSM_CONV):
            acc = acc + ext_ref[base + kk:base + kk + q, c0:c0 + ct] * convw_ref[kk:kk + 1, c0:c0 + ct]
        xs_ref[:, c0:c0 + ct] = _silu(acc + convb_ref[:, c0:c0 + ct])
    ext_ref[0:halo, :] = ext_ref[q:q + halo, :]

    dt = _softplus(dt_ref[...] + dtb_ref[...])
    a = dt * aneg_ref[...]
    row_i = lax.broadcasted_iota(I32, (q, q), 0)
    col_i = lax.broadcasted_iota(I32, (q, q), 1)
    tril = row_i >= col_i
    acum = jnp.dot(tril.astype(F32), a, precision=lax.Precision.HIGHEST,
                   preferred_element_type=F32)
    acum_t = acum.T
    a_last = acum[q - 1:q, :]
    e_in = jnp.exp(acum)
    e_out = jnp.exp(a_last - acum)
    e_last = jnp.exp(a_last)

    lane_lo = lax.broadcasted_iota(I32, (q, LANES), 1) < SSM_HEAD_DIM
    lane_lo_row = lax.broadcasted_iota(I32, (1, LANES), 1) < SSM_HEAD_DIM
    bc_off = d_inner
    cc_off = d_inner + SSM_GROUPS * SSM_STATE

    for g in range(SSM_GROUPS):
        bg = xs_ref[:, bc_off + g * SSM_STATE:bc_off + (g + 1) * SSM_STATE]
        cg = xs_ref[:, cc_off + g * SSM_STATE:cc_off + (g + 1) * SSM_STATE]
        bg16 = bg.astype(BF16)
        cg16 = cg.astype(BF16)
        cb = lax.dot_general(cg16, bg16, (((1,), (1,)), ((), ())), preferred_element_type=F32)
        bgt16 = bg.T.astype(BF16)
        for pi in range(pairs_per_group):
            pp = g * pairs_per_group + pi
            h0, h1 = 2 * pp, 2 * pp + 1
            x_pair = xs_ref[:, pp * LANES:(pp + 1) * LANES]
            xdt = x_pair * jnp.where(lane_lo, dt[:, h0:h0 + 1], dt[:, h1:h1 + 1])
            y_pair = dskip_ref[:, pp * LANES:(pp + 1) * LANES] * x_pair
            for hh, keep in ((h0, lane_lo), (h1, jnp.logical_not(lane_lo))):
                diff = acum[:, hh:hh + 1] - acum_t[hh:hh + 1, :]
                dec = jnp.exp(jnp.where(tril, diff, NEG_BIG))
                m16 = (cb * dec).astype(BF16)
                xm16 = jnp.where(keep, xdt, 0.0).astype(BF16)
                y_pair = y_pair + jnp.dot(m16, xm16, preferred_element_type=F32)
            st = state_ref[pp]
            e_in_pair = jnp.where(lane_lo, e_in[:, h0:h0 + 1], e_in[:, h1:h1 + 1])
            y_pair = y_pair + jnp.dot(cg16, st.astype(BF16), preferred_element_type=F32) * e_in_pair
            e_out_pair = jnp.where(lane_lo, e_out[:, h0:h0 + 1], e_out[:, h1:h1 + 1])
            wx16 = (xdt * e_out_pair).astype(BF16)
            e_last_pair = jnp.where(lane_lo_row, e_last[:, h0:h0 + 1], e_last[:, h1:h1 + 1])
            state_ref[pp] = st * e_last_pair + jnp.dot(bgt16, wx16, preferred_element_type=F32)
            y_ref[:, pp * LANES:(pp + 1) * LANES] = y_pair

    gw = d_inner // SSM_GROUPS
    for g in range(SSM_GROUPS):
        z = zx_ref[:, g * gw:(g + 1) * gw]
        yg = y_ref[:, g * gw:(g + 1) * gw] * _silu(z)
        ms = jnp.mean(yg * yg, axis=-1, keepdims=True)
        o_ref[:, g * gw:(g + 1) * gw] = (yg * lax.rsqrt(ms + EPS) * gnm_ref[:, g * gw:(g + 1) * gw]).astype(o_ref.dtype)


def _ssd_branch(zx, dtk, conv_w, conv_b, dt_bias, a_log, d_skip, gn_m, bsz, seq, q=256):
    t = zx.shape[0]
    d_inner = gn_m.shape[0]
    conv_dim = conv_w.shape[1]
    n_heads = d_inner // SSM_HEAD_DIM
    nc = seq // q
    pad = LANES - n_heads
    dtb = jnp.pad(dt_bias.astype(F32), (0, pad)).reshape(1, LANES)
    aneg = jnp.pad(-jnp.exp(a_log.astype(F32)), (0, pad)).reshape(1, LANES)
    dskip = jnp.repeat(d_skip.astype(F32), SSM_HEAD_DIM).reshape(1, d_inner)
    const = lambda b, c: (0, 0)
    return pl.pallas_call(
        functools.partial(_ssd_kernel, q=q, d_inner=d_inner, conv_dim=conv_dim),
        out_shape=jax.ShapeDtypeStruct((t, d_inner), BF16),
        grid=(bsz, nc),
        in_specs=[pl.BlockSpec((q, d_inner + conv_dim), lambda b, c: (b * nc + c, 0)),
                  pl.BlockSpec((q, LANES), lambda b, c: (b * nc + c, 0)),
                  pl.BlockSpec((SSM_CONV, conv_dim), const),
                  pl.BlockSpec((1, conv_dim), const),
                  pl.BlockSpec((1, LANES), const),
                  pl.BlockSpec((1, LANES), const),
                  pl.BlockSpec((1, d_inner), const),
                  pl.BlockSpec((1, d_inner), const)],
        out_specs=pl.BlockSpec((q, d_inner), lambda b, c: (b * nc + c, 0)),
        scratch_shapes=[pltpu.VMEM((q + 8, conv_dim), F32),
                        pltpu.VMEM((q, conv_dim), F32),
                        pltpu.VMEM((q, d_inner), F32),
                        pltpu.VMEM((n_heads // 2, SSM_STATE, LANES), F32)],
        compiler_params=_cparams(("parallel", "arbitrary")),
        name="ssd_branch",
    )(zx, dtk, conv_w.astype(F32), conv_b.astype(F32).reshape(1, conv_dim), dtb, aneg, dskip,
      gn_m.astype(F32).reshape(1, d_inner))


def _grouped_loop(n, group, body, carry):
    carry = lax.fori_loop(0, n // group, lambda g, cr: body(g * group, group, cr), carry)
    done = (n // group) * group
    size = group // 2
    while size >= 1:
        take = (n // size) % 2
        carry = lax.fori_loop(0, take, lambda _, cr, done=done, size=size: body(done, size, cr), carry)
        done = done + take * size
        size //= 2
    return carry


def _select_kernel(qi_ref, w_ref, kidx_ref, bias_ref, keys_ref, *, tq, tk, topk, idx_scale):
    seq = kidx_ref.shape[0]
    j = pl.program_id(1)
    n_tiles = ((j + 1) * tq + tk - 1) // tk
    lane = lax.broadcasted_iota(I32, (1, tq), 1)
    limit = ((j * tq + lane) // CHUNK + 1) * CHUNK
    row_local = lax.broadcasted_iota(I32, (tk, tq), 0)

    def score_tiles(t0, cnt, carry):
        rows = cnt * tk
        r0 = pl.multiple_of(t0 * tk, tk)
        kb = kidx_ref[pl.ds(r0, rows), :]
        acc = jnp.zeros((rows, tq), F32)
        for h in range(IDX_HEADS):
            s = jnp.dot(kb, qi_ref[h * IDX_HEAD_DIM:(h + 1) * IDX_HEAD_DIM, :],
                        preferred_element_type=F32)
            acc = acc + jnp.maximum(s, 0.0) * w_ref[h:h + 1, :]
        sc = acc * idx_scale + 0.0
        bits = lax.bitcast_convert_type(sc, I32)
        key = jnp.where(bits < 0, bits ^ 0x7FFFFFFF, bits)
        rows_i = lax.broadcasted_iota(I32, (rows, tq), 0) + r0
        keys_ref[pl.ds(r0, rows), :] = jnp.where(rows_i < limit, key, INT_MIN)
        return carry

    _grouped_loop(n_tiles, 2, score_tiles, 0)

    def count(pred):
        def body(kt, acc):
            r0 = pl.multiple_of(kt * tk, tk)
            ones = jnp.where(pred(keys_ref[pl.ds(r0, tk), :], r0), 1, 0).astype(I32)
            return acc + jnp.sum(ones.reshape(tk // 8, 8, tq), axis=0)
        acc = lax.fori_loop(0, n_tiles, body, jnp.zeros((8, tq), I32))
        return jnp.sum(acc, axis=0, keepdims=True)

    def bisect(i, carry):
        thr, c_ge = carry
        cand = thr + jnp.left_shift(jnp.int32(1), 31 - i)
        c = count(lambda kk, r0: kk >= cand)
        ok = c >= topk
        return jnp.where(ok, cand, thr), jnp.where(ok, c, c_ge)

    thr, c_ge = lax.fori_loop(0, 32, bisect, (jnp.full((1, tq), INT_MIN, I32),
                                              jnp.full((1, tq), topk + 1, I32)))
    real = thr > INT_MIN
    tie_excess = jnp.max(jnp.where(jnp.logical_and(real, c_ge > topk), 1, 0)) > 0

    def write_simple():
        thr_eff = jnp.where(real, thr, INT_MIN + 1)

        def write_tile(kt, carry):
            r0 = pl.multiple_of(kt * tk, tk)
            sel = keys_ref[pl.ds(r0, tk), :] >= thr_eff
            bias_ref[pl.ds(r0, tk), :] = jnp.where(sel, 0.0, NEG_BIG).astype(bias_ref.dtype)
            return carry

        lax.fori_loop(0, n_tiles, write_tile, 0)

    def write_ties():
        need = topk - count(lambda kk, r0: kk > thr)
        idx_bits = seq.bit_length()

        def bisect_idx(i, lim):
            cand = lim + jnp.left_shift(jnp.int32(1), idx_bits - 1 - i)
            c = count(lambda kk, r0: jnp.logical_and(kk == thr, row_local + r0 < cand))
            return jnp.where(c <= need, cand, lim)

        lim = lax.fori_loop(0, idx_bits, bisect_idx, jnp.zeros((1, tq), I32))
        lim = jnp.where(real, lim, 0)

        def write_tile(kt, carry):
            r0 = pl.multiple_of(kt * tk, tk)
            kk = keys_ref[pl.ds(r0, tk), :]
            sel = jnp.logical_or(kk > thr, jnp.logical_and(kk == thr, row_local + r0 < lim))
            bias_ref[pl.ds(r0, tk), :] = jnp.where(sel, 0.0, NEG_BIG).astype(bias_ref.dtype)
            return carry

        lax.fori_loop(0, n_tiles, write_tile, 0)

    lax.cond(tie_excess, write_ties, write_simple)

    def fill_tile(kt, carry):
        r0 = pl.multiple_of(kt * tk, tk)
        bias_ref[pl.ds(r0, tk), :] = jnp.full((tk, tq), NEG_BIG, bias_ref.dtype)
        return carry

    lax.fori_loop(n_tiles, seq // tk, fill_tile, 0)


def _select_bias(qi_t, w_t, kidx, bsz, seq, tq, tk):
    nq = seq // tq
    topk = min(TOPK_MAX, seq // 4)
    idx_scale = (IDX_HEAD_DIM ** -0.5) * (IDX_HEADS ** -0.5)
    n_qi = qi_t.shape[1]
    return pl.pallas_call(
        functools.partial(_select_kernel, tq=tq, tk=tk, topk=topk, idx_scale=idx_scale),
        out_shape=jax.ShapeDtypeStruct((bsz, seq, seq), BF16),
        grid=(bsz, nq),
        in_specs=[pl.BlockSpec((None, n_qi, tq), lambda b, j: (b * nq + j, 0, 0)),
                  pl.BlockSpec((None, IDX_HEADS, tq), lambda b, j: (b * nq + j, 0, 0)),
                  pl.BlockSpec((None, seq, IDX_HEAD_DIM), lambda b, j: (b, 0, 0))],
        out_specs=pl.BlockSpec((None, seq, tq), lambda b, j: (b, 0, j)),
        scratch_shapes=[pltpu.VMEM((seq, tq), I32)],
        compiler_params=_cparams(("parallel", "arbitrary")),
        name="topk_select",
    )(qi_t, w_t, kidx)


def _attn_kernel(q_ref, k_ref, v_ref, *rest, tq, tk, heads, n_seg, group, c):
    bias_refs = rest[:n_seg]
    gate_ref, o_ref, s_ref = rest[n_seg:]
    hd = ATTN_HEAD_DIM
    n_tiles = pl.program_id(2) + 1
    tps = bias_refs[0].shape[0] // tk
    qs = [q_ref[i * hd:(i + 1) * hd, :] for i in range(heads)]

    m8 = tuple(jnp.full((8, tq), NEG_BIG, F32) for _ in range(heads))
    for r in range(n_seg):
        def pass1(t0, cnt, m8s, r=r):
            rows = cnt * tk
            r0 = pl.multiple_of((r * tps + t0) * tk, tk)
            b = bias_refs[r][pl.ds(pl.multiple_of(t0 * tk, tk), rows), :].astype(F32)
            out = []
            for i in range(heads):
                kb = k_ref[pl.ds(r0, rows), i * hd:(i + 1) * hd]
                s = jnp.dot(kb, qs[i], preferred_element_type=F32) * c + b
                s_ref[i, pl.ds(r0, rows), :] = s
                out.append(jnp.maximum(m8s[i], jnp.max(s.reshape(rows // 8, 8, tq), axis=0)))
            return tuple(out)

        m8 = _grouped_loop(jnp.clip(n_tiles - r * tps, 0, tps), group, pass1, m8)
    ms = [jnp.max(m, axis=0, keepdims=True) for m in m8]

    def pass2(t0, cnt, accs):
        r0 = pl.multiple_of(t0 * tk, tk)
        ones = jnp.ones((16, cnt * tk), BF16)
        out = []
        for i in range(heads):
            p = jnp.exp2(s_ref[i, pl.ds(r0, cnt * tk), :] - ms[i]).astype(BF16)
            vt = jnp.concatenate([v_ref[t0 + u, i * hd:(i + 1) * hd, :] for u in range(cnt)], axis=1)
            lhs = jnp.concatenate([vt, ones], axis=0)
            out.append(accs[i] + jnp.dot(lhs, p, preferred_element_type=F32))
        return tuple(out)

    accs = _grouped_loop(n_tiles, group, pass2,
                         tuple(jnp.zeros((hd + 16, tq), F32) for _ in range(heads)))
    o = jnp.concatenate([a[:hd] / a[hd:hd + 1] for a in accs], axis=0).T
    o_ref[...] = (o * _silu(gate_ref[...])).astype(o_ref.dtype)


def _attention(q_t, k, v_t, bias, gates, bsz, seq, tq, heads=2, n_seg=2, group=8):
    tk = tq
    nq = seq // tq
    nkt = seq // tk
    d_attn = k.shape[2]
    hw = heads * ATTN_HEAD_DIM
    c = (ATTN_HEAD_DIM ** -0.5) * 1.4426950408889634
    seg = seq // n_seg
    tps = seg // tk
    bias_specs = [pl.BlockSpec((None, seg, tq), lambda b, h, j, r=r: (b, r, jnp.maximum(j, r * tps)))
                  for r in range(n_seg)]
    return pl.pallas_call(
        functools.partial(_attn_kernel, tq=tq, tk=tk, heads=heads, n_seg=n_seg, group=group, c=c),
        out_shape=jax.ShapeDtypeStruct((bsz * seq, d_attn), BF16),
        grid=(bsz, d_attn // hw, nq),
        in_specs=[pl.BlockSpec((None, hw, tq), lambda b, h, j: (b * nq + j, h, 0)),
                  pl.BlockSpec((None, seq, hw), lambda b, h, j: (b, 0, h)),
                  pl.BlockSpec((nkt, hw, tk), lambda b, h, j: (b, h, 0))]
                 + bias_specs
                 + [pl.BlockSpec((tq, hw), lambda b, h, j: (b * nq + j, h))],
        out_specs=pl.BlockSpec((tq, hw), lambda b, h, j: (b * nq + j, h)),
        scratch_shapes=[pltpu.VMEM((heads, seq, tq), F32)],
        compiler_params=_cparams(("parallel", "parallel", "arbitrary")),
        name="masked_attention",
    )(q_t, k, v_t, *([bias] * n_seg), gates)


def _merge_kernel(ym_ref, ya_ref, wm_ref, wa_ref, gm_ref, ga_ref, bm_ref, ba_ref, o_ref):
    o_m = jnp.dot(ym_ref[...], wm_ref[...], preferred_element_type=F32)
    o_a = jnp.dot(ya_ref[...], wa_ref[...], preferred_element_type=F32)
    g_m = jax.nn.sigmoid(gm_ref[...] + bm_ref[...])
    g_a = jax.nn.sigmoid(ga_ref[...] + ba_ref[...])
    o_ref[...] = (g_m * o_m + g_a * o_a).astype(o_ref.dtype)


def _merge(y_m, y_a, w_m, w_a, gates, gate_bias, d_attn, tm=1024, tn=512):
    t, d_inner = y_m.shape
    d_model = w_m.shape[1]
    nb = d_model // tn
    off_m = d_attn // tn
    off_a = off_m + nb
    gb = gate_bias.astype(F32).reshape(1, N_BRANCHES * d_model)
    return pl.pallas_call(
        _merge_kernel,
        out_shape=jax.ShapeDtypeStruct((t, d_model), BF16),
        grid=(t // tm, nb),
        in_specs=[pl.BlockSpec((tm, d_inner), lambda i, j: (i, 0)),
                  pl.BlockSpec((tm, d_attn), lambda i, j: (i, 0)),
                  pl.BlockSpec((d_inner, tn), lambda i, j: (0, j)),
                  pl.BlockSpec((d_attn, tn), lambda i, j: (0, j)),
                  pl.BlockSpec((tm, tn), lambda i, j: (i, off_m + j)),
                  pl.BlockSpec((tm, tn), lambda i, j: (i, off_a + j)),
                  pl.BlockSpec((1, tn), lambda i, j: (0, j)),
                  pl.BlockSpec((1, tn), lambda i, j: (0, nb + j))],
        out_specs=pl.BlockSpec((tm, tn), lambda i, j: (i, j)),
        compiler_params=_cparams(("parallel", "arbitrary")),
        name="branch_merge",
    )(y_m, y_a, w_m, w_a, gates, gates, gb, gb)


def _out_kernel(m_ref, w_ref, x_ref, o_ref):
    o_ref[...] = x_ref[...] + jnp.dot(m_ref[...], w_ref[...], preferred_element_type=F32)


def _out_norm_kernel(m_ref, w_ref, x_ref, g_ref, o_ref):
    y = x_ref[...] + jnp.dot(m_ref[...], w_ref[...], preferred_element_type=F32)
    ms = jnp.mean(y * y, axis=-1, keepdims=True)
    o_ref[...] = y * lax.rsqrt(ms + EPS) * g_ref[...]


def _out_proj(merged, w_out, x2d, norm_g, tm=512):
    t, d = x2d.shape
    row = pl.BlockSpec((tm, d), lambda i: (i, 0))
    in_specs = [row, pl.BlockSpec((d, d), lambda i: (0, 0)), row]
    args = [merged, w_out, x2d]
    kern = _out_kernel
    if norm_g is not None:
        in_specs.append(pl.BlockSpec((1, d), lambda i: (0, 0)))
        args.append(norm_g.astype(F32).reshape(1, d))
        kern = _out_norm_kernel
    return pl.pallas_call(
        kern,
        out_shape=jax.ShapeDtypeStruct((t, d), F32),
        grid=(t // tm,),
        in_specs=in_specs,
        out_specs=row,
        compiler_params=_cparams(("parallel",)),
        name="out_proj",
    )(*args)


def kernel(x, w_in, conv_w, conv_b, dt_bias, a_log, d_skip, gn_m, gate_bias, w_branch, w_out,
           norm_in, norm_final):
    bsz, seq, d_model = x.shape
    depth = w_in.shape[0]
    d_inner = gn_m.shape[1]
    conv_dim = conv_w.shape[2]
    n_heads = dt_bias.shape[1]
    d_attn = w_branch.shape[1] - d_inner
    n_qi = IDX_HEADS * IDX_HEAD_DIM
    t = bsz * seq
    tq_sel, tk_sel = 128, 512
    tq_att = 256
    tm = 1024

    o_z = 0
    o_xbc = o_z + d_inner
    o_dt = o_xbc + conv_dim
    o_q = o_dt + n_heads
    o_k = o_q + d_attn
    o_v = o_k + d_attn
    o_ga = o_v + d_attn
    o_qi = o_ga + d_attn
    o_ki = o_qi + n_qi
    o_wi = o_ki + IDX_HEAD_DIM
    o_g = o_wi + IDX_HEADS
    small_pad = LANES - n_heads - IDX_HEAD_DIM

    x2d = x.reshape(t, d_model)
    for i in range(depth):
        w = w_in[i].astype(BF16)
        w_zx = w[:, o_z:o_dt]
        w_small = jnp.concatenate([w[:, o_dt:o_q], w[:, o_ki:o_wi],
                                   jnp.zeros((d_model, small_pad), BF16)], axis=1)
        w_k = w[:, o_k:o_v]
        w_gates = jnp.concatenate([w[:, o_ga:o_qi], w[:, o_g:]], axis=1)
        wt_q = w[:, o_q:o_k].T
        wt_v = w[:, o_v:o_ga].T
        wt_qi = w[:, o_qi:o_ki].T
        wt_wi = w[:, o_wi:o_g].T

        h = _rmsnorm(x2d, norm_in[i].astype(F32), BF16)
        zx = _matmul_nn(h, w_zx, F32, tm, 512, "proj_zx")
        small = _matmul_nn(h, w_small, F32, tm, LANES, "proj_small")
        k_tok = _matmul_nn(h, w_k, BF16, tm, 512, "proj_k")
        gates = _matmul_nn(h, w_gates, F32, tm, 512, "proj_gates")
        q_t = _matmul_nt(wt_q, h, BF16, tm, 512, tq_att, "proj_qT")
        v_t = _matmul_nt(wt_v, h, BF16, tm, 512, tq_att, "proj_vT")
        qi_t = _matmul_nt(wt_qi, h, BF16, tm, 512, tq_sel, "proj_qidxT")
        wi_t = _matmul_nt(wt_wi, h, F32, tm, IDX_HEADS, tq_sel, "proj_widxT")

        y_m = _ssd_branch(zx, small, conv_w[i], conv_b[i], dt_bias[i], a_log[i], d_skip[i], gn_m[i],
                          bsz, seq)

        kidx = small[:, n_heads:n_heads + IDX_HEAD_DIM].astype(BF16).reshape(bsz, seq, IDX_HEAD_DIM)
        bias = _select_bias(qi_t, wi_t, kidx, bsz, seq, tq_sel, tk_sel)
        y_a = _attention(q_t, k_tok.reshape(bsz, seq, d_attn), v_t, bias, gates, bsz, seq, tq_att)

        wb = w_branch[i].astype(BF16)
        merged = _merge(y_m, y_a, wb[:d_inner], wb[d_inner:], gates, gate_bias[i], d_attn)
        last = i == depth - 1
        x2d = _out_proj(merged, w_out[i].astype(BF16), x2d, norm_final if last else None)
    return x2d.reshape(bsz, seq, d_model)
```

```python
import functools

import jax
import jax.numpy as jnp
from jax import lax
from jax.experimental import pallas as pl
from jax.experimental.pallas import tpu as pltpu

F32 = jnp.float32
BF16 = jnp.bfloat16
I32 = jnp.int32

EPS = 1e-6
CHUNK = 64
SSM_HEAD_DIM = 64
SSM_GROUPS = 4
SSM_STATE = 128
SSM_CONV = 4
ATTN_HEAD_DIM = 128
IDX_HEADS = 16
IDX_HEAD_DIM = 64
TOPK_MAX = 256
N_BRANCHES = 2
ATTN_LOGIT_SCALE = (ATTN_HEAD_DIM ** -0.5) * 1.4426950408889634

LANES = 128
NEG_BIG = -1e30
INT_MIN = -2 ** 31

VMEM_LIMIT = 56 * 1024 * 1024


def _cparams(sem):
    return pltpu.CompilerParams(dimension_semantics=sem, vmem_limit_bytes=VMEM_LIMIT)


def _rmsnorm_kernel(x_ref, g_ref, o_ref):
    x = x_ref[...]
    ms = jnp.mean(x * x, axis=-1, keepdims=True)
    o_ref[...] = (x * lax.rsqrt(ms + EPS) * g_ref[...]).astype(o_ref.dtype)


def _rmsnorm(x2d, g, out_dtype, tm=512):
    t, d = x2d.shape
    return pl.pallas_call(
        _rmsnorm_kernel,
        out_shape=jax.ShapeDtypeStruct((t, d), out_dtype),
        grid=(t // tm,),
        in_specs=[pl.BlockSpec((tm, d), lambda i: (i, 0)),
                  pl.BlockSpec((1, d), lambda i: (0, 0))],
        out_specs=pl.BlockSpec((tm, d), lambda i: (i, 0)),
        compiler_params=_cparams(("parallel",)),
        name="rmsnorm",
    )(x2d, g.reshape(1, d))


def _matmul_nn_kernel(a_ref, b_ref, o_ref):
    o_ref[...] = jnp.dot(a_ref[...], b_ref[...], preferred_element_type=F32).astype(o_ref.dtype)


def _matmul_nn(a, b, out_dtype, tm, tn, name):
    t, k = a.shape
    n = b.shape[1]
    return pl.pallas_call(
        _matmul_nn_kernel,
        out_shape=jax.ShapeDtypeStruct((t, n), out_dtype),
        grid=(t // tm, n // tn),
        in_specs=[pl.BlockSpec((tm, k), lambda i, j: (i, 0)),
                  pl.BlockSpec((k, tn), lambda i, j: (0, j))],
        out_specs=pl.BlockSpec((tm, tn), lambda i, j: (i, j)),
        compiler_params=_cparams(("parallel", "arbitrary")),
        name=name,
    )(a, b)


def _matmul_nt_kernel(w_ref, a_ref, o_ref, *, tt, scale):
    res = lax.dot_general(w_ref[...], a_ref[...], (((1,), (1,)), ((), ())),
                          preferred_element_type=F32)
    if scale is not None:
        res = res * scale
    res = res.astype(o_ref.dtype)
    for c in range(o_ref.shape[0]):
        o_ref[c] = res[:, c * tt:(c + 1) * tt]


def _matmul_nt(wt, a, out_dtype, tm, tn, tt, name, scale=None):
    t, k = a.shape
    n = wt.shape[0]
    return pl.pallas_call(
        functools.partial(_matmul_nt_kernel, tt=tt, scale=scale),
        out_shape=jax.ShapeDtypeStruct((t // tt, n, tt), out_dtype),
        grid=(t // tm, n // tn),
        in_specs=[pl.BlockSpec((tn, k), lambda i, j: (j, 0)),
                  pl.BlockSpec((tm, k), lambda i, j: (i, 0))],
        out_specs=pl.BlockSpec((tm // tt, tn, tt), lambda i, j: (i, j, 0)),
        compiler_params=_cparams(("parallel", "arbitrary")),
        name=name,
    )(wt, a)


def _softplus(x):
    return jnp.maximum(x, 0.0) + jnp.log1p(jnp.exp(-jnp.abs(x)))


def _silu(x):
    return x * jax.nn.sigmoid(x)


def _ssd_kernel(zx_ref, dt_ref, convw_ref, convb_ref, dtb_ref, aneg_ref, dskip_ref, gnm_ref,
                o_ref, ext_ref, xs_ref, y_ref, state_ref, *, q, d_inner, conv_dim):
    n_heads = d_inner // SSM_HEAD_DIM
    n_pairs = n_heads // 2
    pairs_per_group = n_pairs // SSM_GROUPS
    halo = 8

    @pl.when(pl.program_id(1) == 0)
    def _():
        ext_ref[0:halo, :] = jnp.zeros((halo, conv_dim), F32)
        state_ref[...] = jnp.zeros(state_ref.shape, F32)

    ext_ref[halo:halo + q, :] = zx_ref[:, d_inner:d_inner + conv_dim]
    ct = 512
    for c0 in range(0, conv_dim, ct):
        base = halo - (SSM_CONV - 1)
        acc = ext_ref[base:base + q, c0:c0 + ct] * convw_ref[0:1, c0:c0 + ct]
        for kk in range(1, SSM_CONV):
            acc = acc + ext_ref[base + kk:base + kk + q, c0:c0 + ct] * convw_ref[kk:kk + 1, c0:c0 + ct]
        xs_ref[:, c0:c0 + ct] = _silu(acc + convb_ref[:, c0:c0 + ct])
    ext_ref[0:halo, :] = ext_ref[q:q + halo, :]

    dt = _softplus(dt_ref[...] + dtb_ref[...])
    a = dt * aneg_ref[...]
    row_i = lax.broadcasted_iota(I32, (q, q), 0)
    col_i = lax.broadcasted_iota(I32, (q, q), 1)
    tril = row_i >= col_i
    acum = jnp.dot(tril.astype(F32), a, precision=lax.Precision.HIGHEST,
                   preferred_element_type=F32)
    acum_t = acum.T
    a_last = acum[q - 1:q, :]
    e_in = jnp.exp(acum)
    e_out = jnp.exp(a_last - acum)
    e_last = jnp.exp(a_last)

    lane_lo = lax.broadcasted_iota(I32, (q, LANES), 1) < SSM_HEAD_DIM
    lane_lo_row = lax.broadcasted_iota(I32, (1, LANES), 1) < SSM_HEAD_DIM
    bc_off = d_inner
    cc_off = d_inner + SSM_GROUPS * SSM_STATE

    for g in range(SSM_GROUPS):
        bg = xs_ref[:, bc_off + g * SSM_STATE:bc_off + (g + 1) * SSM_STATE]
        cg = xs_ref[:, cc_off + g * SSM_STATE:cc_off + (g + 1) * SSM_STATE]
        bg16 = bg.astype(BF16)
        cg16 = cg.astype(BF16)
        cb = lax.dot_general(cg16, bg16, (((1,), (1,)), ((), ())), preferred_element_type=F32)
        bgt16 = bg.T.astype(BF16)
        for pi in range(pairs_per_group):
            pp = g * pairs_per_group + pi
            h0, h1 = 2 * pp, 2 * pp + 1
            x_pair = xs_ref[:, pp * LANES:(pp + 1) * LANES]
            xdt = x_pair * jnp.where(lane_lo, dt[:, h0:h0 + 1], dt[:, h1:h1 + 1])
            y_pair = dskip_ref[:, pp * LANES:(pp + 1) * LANES] * x_pair
            for hh, keep in ((h0, lane_lo), (h1, jnp.logical_not(lane_lo))):
                diff = acum[:, hh:hh + 1] - acum_t[hh:hh + 1, :]
                dec = jnp.exp(jnp.where(tril, diff, NEG_BIG))
                m16 = (cb * dec).astype(BF16)
                xm16 = jnp.where(keep, xdt, 0.0).astype(BF16)
                y_pair = y_pair + jnp.dot(m16, xm16, preferred_element_type=F32)
            st = state_ref[pp]
            e_in_pair = jnp.where(lane_lo, e_in[:, h0:h0 + 1], e_in[:, h1:h1 + 1])
            y_pair = y_pair + jnp.dot(cg16, st.astype(BF16), preferred_element_type=F32) * e_in_pair
            e_out_pair = jnp.where(lane_lo, e_out[:, h0:h0 + 1], e_out[:, h1:h1 + 1])
            wx16 = (xdt * e_out_pair).astype(BF16)
            e_last_pair = jnp.where(lane_lo_row, e_last[:, h0:h0 + 1], e_last[:, h1:h1 + 1])
            state_ref[pp] = st * e_last_pair + jnp.dot(bgt16, wx16, preferred_element_type=F32)
            y_ref[:, pp * LANES:(pp + 1) * LANES] = y_pair

    gw = d_inner // SSM_GROUPS
    for g in range(SSM_GROUPS):
        z = zx_ref[:, g * gw:(g + 1) * gw]
        yg = y_ref[:, g * gw:(g + 1) * gw] * _silu(z)
        ms = jnp.mean(yg * yg, axis=-1, keepdims=True)
        o_ref[:, g * gw:(g + 1) * gw] = (yg * lax.rsqrt(ms + EPS) * gnm_ref[:, g * gw:(g + 1) * gw]).astype(o_ref.dtype)


def _ssd_branch(zx, dtk, conv_w, conv_b, dt_bias, a_log, d_skip, gn_m, bsz, seq, q=256):
    t = zx.shape[0]
    d_inner = gn_m.shape[0]
    conv_dim = conv_w.shape[1]
    n_heads = d_inner // SSM_HEAD_DIM
    nc = seq // q
    pad = LANES - n_heads
    dtb = jnp.pad(dt_bias.astype(F32), (0, pad)).reshape(1, LANES)
    aneg = jnp.pad(-jnp.exp(a_log.astype(F32)), (0, pad)).reshape(1, LANES)
    dskip = jnp.repeat(d_skip.astype(F32), SSM_HEAD_DIM).reshape(1, d_inner)
    const = lambda b, c: (0, 0)
    return pl.pallas_call(
        functools.partial(_ssd_kernel, q=q, d_inner=d_inner, conv_dim=conv_dim),
        out_shape=jax.ShapeDtypeStruct((t, d_inner), BF16),
        grid=(bsz, nc),
        in_specs=[pl.BlockSpec((q, d_inner + conv_dim), lambda b, c: (b * nc + c, 0)),
                  pl.BlockSpec((q, LANES), lambda b, c: (b * nc + c, 0)),
                  pl.BlockSpec((SSM_CONV, conv_dim), const),
                  pl.BlockSpec((1, conv_dim), const),
                  pl.BlockSpec((1, LANES), const),
                  pl.BlockSpec((1, LANES), const),
                  pl.BlockSpec((1, d_inner), const),
                  pl.BlockSpec((1, d_inner), const)],
        out_specs=pl.BlockSpec((q, d_inner), lambda b, c: (b * nc + c, 0)),
        scratch_shapes=[pltpu.VMEM((q + 8, conv_dim), F32),
                        pltpu.VMEM((q, conv_dim), F32),
                        pltpu.VMEM((q, d_inner), F32),
                        pltpu.VMEM((n_heads // 2, SSM_STATE, LANES), F32)],
        compiler_params=_cparams(("parallel", "arbitrary")),
        name="ssd_branch",
    )(zx, dtk, conv_w.astype(F32), conv_b.astype(F32).reshape(1, conv_dim), dtb, aneg, dskip,
      gn_m.astype(F32).reshape(1, d_inner))


def _grouped_loop(n, group, body, carry):
    carry = lax.fori_loop(0, n // group, lambda g, cr: body(g * group, group, cr), carry)
    done = (n // group) * group
    size = group // 2
    while size >= 1:
        take = (n // size) % 2
        carry = lax.fori_loop(0, take, lambda _, cr, done=done, size=size: body(done, size, cr), carry)
        done = done + take * size
        size //= 2
    return carry


def _select_kernel(qi_ref, w_ref, kidx_ref, bias_ref, keys_ref, *, tq, tk, topk, idx_scale):
    seq = kidx_ref.shape[0]
    j = pl.program_id(1)
    n_tiles = ((j + 1) * tq + tk - 1) // tk
    lane = lax.broadcasted_iota(I32, (1, tq), 1)
    limit = ((j * tq + lane) // CHUNK + 1) * CHUNK
    row_local = lax.broadcasted_iota(I32, (tk, tq), 0)

    def score_tiles(t0, cnt, carry):
        rows = cnt * tk
        r0 = pl.multiple_of(t0 * tk, tk)
        kb = kidx_ref[pl.ds(r0, rows), :]
        acc = jnp.zeros((rows, tq), F32)
        for h in range(IDX_HEADS):
            s = jnp.dot(kb, qi_ref[h * IDX_HEAD_DIM:(h + 1) * IDX_HEAD_DIM, :],
                        preferred_element_type=F32)
            acc = acc + jnp.maximum(s, 0.0) * w_ref[h:h + 1, :]
        sc = acc * idx_scale + 0.0
        bits = lax.bitcast_convert_type(sc, I32)
        key = jnp.where(bits < 0, bits ^ 0x7FFFFFFF, bits)
        rows_i = lax.broadcasted_iota(I32, (rows, tq), 0) + r0
        keys_ref[pl.ds(r0, rows), :] = jnp.where(rows_i < limit, key, INT_MIN)
        return carry

    _grouped_loop(n_tiles, 2, score_tiles, 0)

    def count(pred):
        def body(kt, acc):
            r0 = pl.multiple_of(kt * tk, tk)
            ones = jnp.where(pred(keys_ref[pl.ds(r0, tk), :], r0), 1, 0).astype(I32)
            return acc + jnp.sum(ones.reshape(tk // 8, 8, tq), axis=0)
        acc = lax.fori_loop(0, n_tiles, body, jnp.zeros((8, tq), I32))
        return jnp.sum(acc, axis=0, keepdims=True)

    def bisect(i, carry):
        thr, c_ge = carry
        cand = thr + jnp.left_shift(jnp.int32(1), 31 - i)
        c = count(lambda kk, r0: kk >= cand)
        ok = c >= topk
        return jnp.where(ok, cand, thr), jnp.where(ok, c, c_ge)

    thr, c_ge = lax.fori_loop(0, 32, bisect, (jnp.full((1, tq), INT_MIN, I32),
                                              jnp.full((1, tq), topk + 1, I32)))
    real = thr > INT_MIN
    tie_excess = jnp.max(jnp.where(jnp.logical_and(real, c_ge > topk), 1, 0)) > 0

    def write_simple():
        thr_eff = jnp.where(real, thr, INT_MIN + 1)

        def write_tile(kt, carry):
            r0 = pl.multiple_of(kt * tk, tk)
            sel = keys_ref[pl.ds(r0, tk), :] >= thr_eff
            bias_ref[pl.ds(r0, tk), :] = jnp.where(sel, 0.0, NEG_BIG).astype(bias_ref.dtype)
            return carry

        lax.fori_loop(0, n_tiles, write_tile, 0)

    def write_ties():
        need = topk - count(lambda kk, r0: kk > thr)
        idx_bits = seq.bit_length()

        def bisect_idx(i, lim):
            cand = lim + jnp.left_shift(jnp.int32(1), idx_bits - 1 - i)
            c = count(lambda kk, r0: jnp.logical_and(kk == thr, row_local + r0 < cand))
            return jnp.where(c <= need, cand, lim)

        lim = lax.fori_loop(0, idx_bits, bisect_idx, jnp.zeros((1, tq), I32))
        lim = jnp.where(real, lim, 0)

        def write_tile(kt, carry):
            r0 = pl.multiple_of(kt * tk, tk)
            kk = keys_ref[pl.ds(r0, tk), :]
            sel = jnp.logical_or(kk > thr, jnp.logical_and(kk == thr, row_local + r0 < lim))
            bias_ref[pl.ds(r0, tk), :] = jnp.where(sel, 0.0, NEG_BIG).astype(bias_ref.dtype)
            return carry

        lax.fori_loop(0, n_tiles, write_tile, 0)

    lax.cond(tie_excess, write_ties, write_simple)

    def fill_tile(kt, carry):
        r0 = pl.multiple_of(kt * tk, tk)
        bias_ref[pl.ds(r0, tk), :] = jnp.full((tk, tq), NEG_BIG, bias_ref.dtype)
        return carry

    lax.fori_loop(n_tiles, seq // tk, fill_tile, 0)


def _select_bias(qi_t, w_t, kidx, bsz, seq, tq, tk):
    nq = seq // tq
    topk = min(TOPK_MAX, seq // 4)
    idx_scale = (IDX_HEAD_DIM ** -0.5) * (IDX_HEADS ** -0.5)
    n_qi = qi_t.shape[1]
    return pl.pallas_call(
        functools.partial(_select_kernel, tq=tq, tk=tk, topk=topk, idx_scale=idx_scale),
        out_shape=jax.ShapeDtypeStruct((bsz, seq, seq), BF16),
        grid=(bsz, nq),
        in_specs=[pl.BlockSpec((None, n_qi, tq), lambda b, j: (b * nq + j, 0, 0)),
                  pl.BlockSpec((None, IDX_HEADS, tq), lambda b, j: (b * nq + j, 0, 0)),
                  pl.BlockSpec((None, seq, IDX_HEAD_DIM), lambda b, j: (b, 0, 0))],
        out_specs=pl.BlockSpec((None, seq, tq), lambda b, j: (b, 0, j)),
        scratch_shapes=[pltpu.VMEM((seq, tq), I32)],
        compiler_params=_cparams(("parallel", "arbitrary")),
        name="topk_select",
    )(qi_t, w_t, kidx)


def _attn_kernel(q_ref, k_ref, v_ref, bias_ref, gate_ref, o_ref, s_ref, mc_ref, m_ref, acc_ref,
                 *, tq, tk, heads, group):
    hd = ATTN_HEAD_DIM
    n_tiles = pl.program_id(2) + 1
    qs = [q_ref[i * hd:(i + 1) * hd, :] for i in range(heads)]

    def logits(t0, cnt, slot):
        rows = cnt * tk
        r0 = t0 * tk if isinstance(t0, int) else pl.multiple_of(t0 * tk, tk)
        b = bias_ref[pl.ds(r0, rows), :].astype(F32)
        for i in range(heads):
            kb = k_ref[pl.ds(r0, rows), i * hd:(i + 1) * hd]
            s = jnp.dot(kb, qs[i], preferred_element_type=F32) + b
            s_ref[slot, i, pl.ds(0, rows), :] = s
            mc_ref[slot, i] = jnp.max(s.reshape(rows // 8, 8, tq), axis=0)

    def absorb(t0, cnt, slot):
        rows = cnt * tk
        ones = jnp.ones((16, rows), BF16)
        for i in range(heads):
            m = m_ref[i]
            m_new = jnp.maximum(m, jnp.max(mc_ref[slot, i], axis=0, keepdims=True))
            p = jnp.exp2(s_ref[slot, i, pl.ds(0, rows), :] - m_new).astype(BF16)
            vt = jnp.concatenate([v_ref[t0 + u, i * hd:(i + 1) * hd, :] for u in range(cnt)], axis=1)
            lhs = jnp.concatenate([vt, ones], axis=0)
            acc_ref[i] = jnp.exp2(m - m_new) * acc_ref[i] + jnp.dot(lhs, p, preferred_element_type=F32)
            m_ref[i] = m_new

    m_ref[...] = jnp.full(m_ref.shape, NEG_BIG, F32)
    acc_ref[...] = jnp.zeros(acc_ref.shape, F32)
    n_full = n_tiles // group
    max_full = (k_ref.shape[0] // tk) // group

    for ch in range(max_full + 1):
        if ch < max_full:
            @pl.when(ch < n_full)
            def _(ch=ch):
                logits(ch * group, group, ch % 2)
                if ch >= 1:
                    absorb((ch - 1) * group, group, (ch - 1) % 2)
        if ch >= 1:
            @pl.when(ch == n_full)
            def _(ch=ch):
                absorb((ch - 1) * group, group, (ch - 1) % 2)

    done = n_full * group
    size = group // 2
    while size >= 1:
        take = (n_tiles // size) % 2

        @pl.when(take == 1)
        def _(done=done, size=size):
            logits(done, size, 0)
            absorb(done, size, 0)

        done = done + take * size
        size //= 2
    o = jnp.concatenate([acc_ref[i, 0:hd, :] / acc_ref[i, hd:hd + 1, :] for i in range(heads)], axis=0).T
    o_ref[...] = (o * _silu(gate_ref[...])).astype(o_ref.dtype)


def _attention(q_t, k, v_t, bias, gates, bsz, seq, tq, heads=2, group=8):
    tk = tq
    nq = seq // tq
    nkt = seq // tk
    d_attn = k.shape[2]
    hw = heads * ATTN_HEAD_DIM
    return pl.pallas_call(
        functools.partial(_attn_kernel, tq=tq, tk=tk, heads=heads, group=group),
        out_shape=jax.ShapeDtypeStruct((bsz * seq, d_attn), BF16),
        grid=(bsz, d_attn // hw, nq),
        in_specs=[pl.BlockSpec((None, hw, tq), lambda b, h, j: (b * nq + j, h, 0)),
                  pl.BlockSpec((None, seq, hw), lambda b, h, j: (b, 0, h)),
                  pl.BlockSpec((nkt, hw, tk), lambda b, h, j: (b, h, 0)),
                  pl.BlockSpec((None, seq, tq), lambda b, h, j: (b, 0, j)),
                  pl.BlockSpec((tq, hw), lambda b, h, j: (b * nq + j, h))],
        out_specs=pl.BlockSpec((tq, hw), lambda b, h, j: (b * nq + j, h)),
        scratch_shapes=[pltpu.VMEM((2, heads, group * tk, tq), F32),
                        pltpu.VMEM((2, heads, 8, tq), F32),
                        pltpu.VMEM((heads, 1, tq), F32),
                        pltpu.VMEM((heads, ATTN_HEAD_DIM + 16, tq), F32)],
        compiler_params=_cparams(("parallel", "parallel", "arbitrary")),
        name="masked_attention",
    )(q_t, k, v_t, bias, gates)


def _merge_kernel(ym_ref, ya_ref, wm_ref, wa_ref, gm_ref, ga_ref, bm_ref, ba_ref, o_ref):
    o_m = jnp.dot(ym_ref[...], wm_ref[...], preferred_element_type=F32)
    o_a = jnp.dot(ya_ref[...], wa_ref[...], preferred_element_type=F32)
    g_m = jax.nn.sigmoid(gm_ref[...] + bm_ref[...])
    g_a = jax.nn.sigmoid(ga_ref[...] + ba_ref[...])
    o_ref[...] = (g_m * o_m + g_a * o_a).astype(o_ref.dtype)


def _merge(y_m, y_a, w_m, w_a, gates, gate_bias, d_attn, tm=1024, tn=512):
    t, d_inner = y_m.shape
    d_model = w_m.shape[1]
    nb = d_model // tn
    off_m = d_attn // tn
    off_a = off_m + nb
    gb = gate_bias.astype(F32).reshape(1, N_BRANCHES * d_model)
    return pl.pallas_call(
        _merge_kernel,
        out_shape=jax.ShapeDtypeStruct((t, d_model), BF16),
        grid=(t // tm, nb),
        in_specs=[pl.BlockSpec((tm, d_inner), lambda i, j: (i, 0)),
                  pl.BlockSpec((tm, d_attn), lambda i, j: (i, 0)),
                  pl.BlockSpec((d_inner, tn), lambda i, j: (0, j)),
                  pl.BlockSpec((d_attn, tn), lambda i, j: (0, j)),
                  pl.BlockSpec((tm, tn), lambda i, j: (i, off_m + j)),
                  pl.BlockSpec((tm, tn), lambda i, j: (i, off_a + j)),
                  pl.BlockSpec((1, tn), lambda i, j: (0, j)),
                  pl.BlockSpec((1, tn), lambda i, j: (0, nb + j))],
        out_specs=pl.BlockSpec((tm, tn), lambda i, j: (i, j)),
        compiler_params=_cparams(("parallel", "arbitrary")),
        name="branch_merge",
    )(y_m, y_a, w_m, w_a, gates, gates, gb, gb)


def _out_kernel(m_ref, w_ref, x_ref, o_ref):
    o_ref[...] = x_ref[...] + jnp.dot(m_ref[...], w_ref[...], preferred_element_type=F32)


def _out_norm_kernel(m_ref, w_ref, x_ref, g_ref, o_ref):
    y = x_ref[...] + jnp.dot(m_ref[...], w_ref[...], preferred_element_type=F32)
    ms = jnp.mean(y * y, axis=-1, keepdims=True)
    o_ref[...] = y * lax.rsqrt(ms + EPS) * g_ref[...]


def _out_proj(merged, w_out, x2d, norm_g, tm=512):
    t, d = x2d.shape
    row = pl.BlockSpec((tm, d), lambda i: (i, 0))
    in_specs = [row, pl.BlockSpec((d, d), lambda i: (0, 0)), row]
    args = [merged, w_out, x2d]
    kern = _out_kernel
    if norm_g is not None:
        in_specs.append(pl.BlockSpec((1, d), lambda i: (0, 0)))
        args.append(norm_g.astype(F32).reshape(1, d))
        kern = _out_norm_kernel
    return pl.pallas_call(
        kern,
        out_shape=jax.ShapeDtypeStruct((t, d), F32),
        grid=(t // tm,),
        in_specs=in_specs,
        out_specs=row,
        compiler_params=_cparams(("parallel",)),
        name="out_proj",
    )(*args)


def kernel(x, w_in, conv_w, conv_b, dt_bias, a_log, d_skip, gn_m, gate_bias, w_branch, w_out,
           norm_in, norm_final):
    bsz, seq, d_model = x.shape
    depth = w_in.shape[0]
    d_inner = gn_m.shape[1]
    conv_dim = conv_w.shape[2]
    n_heads = dt_bias.shape[1]
    d_attn = w_branch.shape[1] - d_inner
    n_qi = IDX_HEADS * IDX_HEAD_DIM
    t = bsz * seq
    tq_sel, tk_sel = 128, 512
    tq_att = 256
    tm = 1024

    o_z = 0
    o_xbc = o_z + d_inner
    o_dt = o_xbc + conv_dim
    o_q = o_dt + n_heads
    o_k = o_q + d_attn
    o_v = o_k + d_attn
    o_ga = o_v + d_attn
    o_qi = o_ga + d_attn
    o_ki = o_qi + n_qi
    o_wi = o_ki + IDX_HEAD_DIM
    o_g = o_wi + IDX_HEADS
    small_pad = LANES - n_heads - IDX_HEAD_DIM

    x2d = x.reshape(t, d_model)
    for i in range(depth):
        w = w_in[i].astype(BF16)
        w_zx = w[:, o_z:o_dt]
        w_small = jnp.concatenate([w[:, o_dt:o_q], w[:, o_ki:o_wi],
                                   jnp.zeros((d_model, small_pad), BF16)], axis=1)
        w_k = w[:, o_k:o_v]
        w_gates = jnp.concatenate([w[:, o_ga:o_qi], w[:, o_g:]], axis=1)
        wt_q = w[:, o_q:o_k].T
        wt_v = w[:, o_v:o_ga].T
        wt_qi = w[:, o_qi:o_ki].T
        wt_wi = w[:, o_wi:o_g].T

        h = _rmsnorm(x2d, norm_in[i].astype(F32), BF16)
        zx = _matmul_nn(h, w_zx, F32, tm, 512, "proj_zx")
        small = _matmul_nn(h, w_small, F32, tm, LANES, "proj_small")
        k_tok = _matmul_nn(h, w_k, BF16, tm, 512, "proj_k")
        gates = _matmul_nn(h, w_gates, F32, tm, 512, "proj_gates")
        q_t = _matmul_nt(wt_q, h, BF16, tm, 512, tq_att, "proj_qT", scale=ATTN_LOGIT_SCALE)
        v_t = _matmul_nt(wt_v, h, BF16, tm, 512, tq_att, "proj_vT")
        qi_t = _matmul_nt(wt_qi, h, BF16, tm, 512, tq_sel, "proj_qidxT")
        wi_t = _matmul_nt(wt_wi, h, F32, tm, IDX_HEADS, tq_sel, "proj_widxT")

        y_m = _ssd_branch(zx, small, conv_w[i], conv_b[i], dt_bias[i], a_log[i], d_skip[i], gn_m[i],
                          bsz, seq)

        kidx = small[:, n_heads:n_heads + IDX_HEAD_DIM].astype(BF16).reshape(bsz, seq, IDX_HEAD_DIM)
        bias = _select_bias(qi_t, wi_t, kidx, bsz, seq, tq_sel, tk_sel)
        y_a = _attention(q_t, k_tok.reshape(bsz, seq, d_attn), v_t, bias, gates, bsz, seq, tq_att)

        wb = w_branch[i].astype(BF16)
        merged = _merge(y_m, y_a, wb[:d_inner], wb[d_inner:], gates, gate_bias[i], d_attn)
        last = i == depth - 1
        x2d = _out_proj(merged, w_out[i].astype(BF16), x2d, norm_final if last else None)
    return x2d.reshape(bsz, seq, d_model)
```

```python
import functools

import jax
import jax.numpy as jnp
from jax import lax
from jax.experimental import pallas as pl
from jax.experimental.pallas import tpu as pltpu

F32 = jnp.float32
BF16 = jnp.bfloat16
I32 = jnp.int32

EPS = 1e-6
CHUNK = 64
SSM_HEAD_DIM = 64
SSM_GROUPS = 4
SSM_STATE = 128
SSM_CONV = 4
ATTN_HEAD_DIM = 128
IDX_HEADS = 16
IDX_HEAD_DIM = 64
TOPK_MAX = 256
N_BRANCHES = 2
LOG2_E = 1.4426950408889634
ATTN_LOGIT_SCALE = (ATTN_HEAD_DIM ** -0.5) * LOG2_E

LANES = 128
NEG_BIG = -1e30
INT_MIN = -2 ** 31

VMEM_LIMIT = 56 * 1024 * 1024


def _cparams(sem):
    return pltpu.CompilerParams(dimension_semantics=sem, vmem_limit_bytes=VMEM_LIMIT)


def _rmsnorm_kernel(x_ref, g_ref, o_ref):
    x = x_ref[...]
    ms = jnp.mean(x * x, axis=-1, keepdims=True)
    o_ref[...] = (x * lax.rsqrt(ms + EPS) * g_ref[...]).astype(o_ref.dtype)


def _rmsnorm(x2d, g, out_dtype, tm=512):
    t, d = x2d.shape
    return pl.pallas_call(
        _rmsnorm_kernel,
        out_shape=jax.ShapeDtypeStruct((t, d), out_dtype),
        grid=(t // tm,),
        in_specs=[pl.BlockSpec((tm, d), lambda i: (i, 0)),
                  pl.BlockSpec((1, d), lambda i: (0, 0))],
        out_specs=pl.BlockSpec((tm, d), lambda i: (i, 0)),
        compiler_params=_cparams(("parallel",)),
        name="rmsnorm",
    )(x2d, g.reshape(1, d))


def _matmul_nn_kernel(a_ref, b_ref, o_ref):
    o_ref[...] = jnp.dot(a_ref[...], b_ref[...], preferred_element_type=F32).astype(o_ref.dtype)


def _matmul_nn(a, b, out_dtype, tm, tn, name):
    t, k = a.shape
    n = b.shape[1]
    return pl.pallas_call(
        _matmul_nn_kernel,
        out_shape=jax.ShapeDtypeStruct((t, n), out_dtype),
        grid=(t // tm, n // tn),
        in_specs=[pl.BlockSpec((tm, k), lambda i, j: (i, 0)),
                  pl.BlockSpec((k, tn), lambda i, j: (0, j))],
        out_specs=pl.BlockSpec((tm, tn), lambda i, j: (i, j)),
        compiler_params=_cparams(("parallel", "arbitrary")),
        name=name,
    )(a, b)


def _matmul_nt_kernel(w_ref, a_ref, o_ref, *, tt, scale):
    res = lax.dot_general(w_ref[...], a_ref[...], (((1,), (1,)), ((), ())),
                          preferred_element_type=F32)
    if scale is not None:
        res = res * scale
    res = res.astype(o_ref.dtype)
    for c in range(o_ref.shape[0]):
        o_ref[c] = res[:, c * tt:(c + 1) * tt]


def _matmul_nt(wt, a, out_dtype, tm, tn, tt, name, scale=None):
    t, k = a.shape
    n = wt.shape[0]
    return pl.pallas_call(
        functools.partial(_matmul_nt_kernel, tt=tt, scale=scale),
        out_shape=jax.ShapeDtypeStruct((t // tt, n, tt), out_dtype),
        grid=(t // tm, n // tn),
        in_specs=[pl.BlockSpec((tn, k), lambda i, j: (j, 0)),
                  pl.BlockSpec((tm, k), lambda i, j: (i, 0))],
        out_specs=pl.BlockSpec((tm // tt, tn, tt), lambda i, j: (i, j, 0)),
        compiler_params=_cparams(("parallel", "arbitrary")),
        name=name,
    )(wt, a)


def _softplus(x):
    return jnp.maximum(x, 0.0) + jnp.log1p(jnp.exp(-jnp.abs(x)))


def _silu(x):
    return x * (0.5 * jnp.tanh(0.5 * x) + 0.5)


def _ssd_kernel(zx_ref, dt_ref, convw_ref, convb_ref, dtb_ref, aneg_ref, dskip_ref, gnm_ref,
                o_ref, ext_ref, xs_ref, y_ref, state_ref, *, q, d_inner, conv_dim):
    n_heads = d_inner // SSM_HEAD_DIM
    n_pairs = n_heads // 2
    pairs_per_group = n_pairs // SSM_GROUPS
    halo = 8

    @pl.when(pl.program_id(1) == 0)
    def _():
        ext_ref[0:halo, :] = jnp.zeros((halo, conv_dim), F32)
        state_ref[...] = jnp.zeros(state_ref.shape, F32)

    ext_ref[halo:halo + q, :] = zx_ref[:, d_inner:d_inner + conv_dim]
    ct = 512
    for c0 in range(0, conv_dim, ct):
        base = halo - (SSM_CONV - 1)
        acc = ext_ref[base:base + q, c0:c0 + ct] * convw_ref[0:1, c0:c0 + ct]
        for kk in range(1, SSM_CONV):
            acc = acc + ext_ref[base + kk:base + kk + q, c0:c0 + ct] * convw_ref[kk:kk + 1, c0:c0 + ct]
        xs_ref[:, c0:c0 + ct] = _silu(acc + convb_ref[:, c0:c0 + ct])
    ext_ref[0:halo, :] = ext_ref[q:q + halo, :]

    dt = _softplus(dt_ref[...] + dtb_ref[...])
    a = dt * aneg_ref[...]
    row_i = lax.broadcasted_iota(I32, (q, q), 0)
    col_i = lax.broadcasted_iota(I32, (q, q), 1)
    tril = row_i >= col_i
    acum = jnp.dot(tril.astype(F32), a, precision=lax.Precision.HIGHEST,
                   preferred_element_type=F32) * LOG2_E
    acum_t = acum.T
    a_last = acum[q - 1:q, :]
    e_in = jnp.exp2(acum)
    e_out = jnp.exp2(a_last - acum)
    e_last = jnp.exp2(a_last)

    lane_lo = lax.broadcasted_iota(I32, (q, LANES), 1) < SSM_HEAD_DIM
    lane_lo_row = lax.broadcasted_iota(I32, (1, LANES), 1) < SSM_HEAD_DIM
    bc_off = d_inner
    cc_off = d_inner + SSM_GROUPS * SSM_STATE

    for g in range(SSM_GROUPS):
        bg = xs_ref[:, bc_off + g * SSM_STATE:bc_off + (g + 1) * SSM_STATE]
        cg = xs_ref[:, cc_off + g * SSM_STATE:cc_off + (g + 1) * SSM_STATE]
        bg16 = bg.astype(BF16)
        cg16 = cg.astype(BF16)
        cb = lax.dot_general(cg16, bg16, (((1,), (1,)), ((), ())), preferred_element_type=F32)
        bgt16 = bg.T.astype(BF16)
        for pi in range(pairs_per_group):
            pp = g * pairs_per_group + pi
            h0, h1 = 2 * pp, 2 * pp + 1
            x_pair = xs_ref[:, pp * LANES:(pp + 1) * LANES]
            xdt = x_pair * jnp.where(lane_lo, dt[:, h0:h0 + 1], dt[:, h1:h1 + 1])
            y_pair = dskip_ref[:, pp * LANES:(pp + 1) * LANES] * x_pair
            for hh, keep in ((h0, lane_lo), (h1, jnp.logical_not(lane_lo))):
                diff = acum[:, hh:hh + 1] - acum_t[hh:hh + 1, :]
                dec = jnp.exp2(jnp.where(tril, diff, NEG_BIG))
                m16 = (cb * dec).astype(BF16)
                xm16 = jnp.where(keep, xdt, 0.0).astype(BF16)
                y_pair = y_pair + jnp.dot(m16, xm16, preferred_element_type=F32)
            st = state_ref[pp]
            e_in_pair = jnp.where(lane_lo, e_in[:, h0:h0 + 1], e_in[:, h1:h1 + 1])
            y_pair = y_pair + jnp.dot(cg16, st.astype(BF16), preferred_element_type=F32) * e_in_pair
            e_out_pair = jnp.where(lane_lo, e_out[:, h0:h0 + 1], e_out[:, h1:h1 + 1])
            wx16 = (xdt * e_out_pair).astype(BF16)
            e_last_pair = jnp.where(lane_lo_row, e_last[:, h0:h0 + 1], e_last[:, h1:h1 + 1])
            state_ref[pp] = st * e_last_pair + jnp.dot(bgt16, wx16, preferred_element_type=F32)
            y_ref[:, pp * LANES:(pp + 1) * LANES] = y_pair

    gw = d_inner // SSM_GROUPS
    for g in range(SSM_GROUPS):
        z = zx_ref[:, g * gw:(g + 1) * gw]
        yg = y_ref[:, g * gw:(g + 1) * gw] * _silu(z)
        ms = jnp.mean(yg * yg, axis=-1, keepdims=True)
        o_ref[:, g * gw:(g + 1) * gw] = (yg * lax.rsqrt(ms + EPS) * gnm_ref[:, g * gw:(g + 1) * gw]).astype(o_ref.dtype)


def _ssd_branch(zx, dtk, conv_w, conv_b, dt_bias, a_log, d_skip, gn_m, bsz, seq, q=128):
    t = zx.shape[0]
    d_inner = gn_m.shape[0]
    conv_dim = conv_w.shape[1]
    n_heads = d_inner // SSM_HEAD_DIM
    nc = seq // q
    pad = LANES - n_heads
    dtb = jnp.pad(dt_bias.astype(F32), (0, pad)).reshape(1, LANES)
    aneg = jnp.pad(-jnp.exp(a_log.astype(F32)), (0, pad)).reshape(1, LANES)
    dskip = jnp.repeat(d_skip.astype(F32), SSM_HEAD_DIM).reshape(1, d_inner)
    const = lambda b, c: (0, 0)
    return pl.pallas_call(
        functools.partial(_ssd_kernel, q=q, d_inner=d_inner, conv_dim=conv_dim),
        out_shape=jax.ShapeDtypeStruct((t, d_inner), BF16),
        grid=(bsz, nc),
        in_specs=[pl.BlockSpec((q, d_inner + conv_dim), lambda b, c: (b * nc + c, 0)),
                  pl.BlockSpec((q, LANES), lambda b, c: (b * nc + c, 0)),
                  pl.BlockSpec((SSM_CONV, conv_dim), const),
                  pl.BlockSpec((1, conv_dim), const),
                  pl.BlockSpec((1, LANES), const),
                  pl.BlockSpec((1, LANES), const),
                  pl.BlockSpec((1, d_inner), const),
                  pl.BlockSpec((1, d_inner), const)],
        out_specs=pl.BlockSpec((q, d_inner), lambda b, c: (b * nc + c, 0)),
        scratch_shapes=[pltpu.VMEM((q + 8, conv_dim), F32),
                        pltpu.VMEM((q, conv_dim), F32),
                        pltpu.VMEM((q, d_inner), F32),
                        pltpu.VMEM((n_heads // 2, SSM_STATE, LANES), F32)],
        compiler_params=_cparams(("parallel", "arbitrary")),
        name="ssd_branch",
    )(zx, dtk, conv_w.astype(F32), conv_b.astype(F32).reshape(1, conv_dim), dtb, aneg, dskip,
      gn_m.astype(F32).reshape(1, d_inner))


def _grouped_loop(n, group, body, carry):
    carry = lax.fori_loop(0, n // group, lambda g, cr: body(g * group, group, cr), carry)
    done = (n // group) * group
    size = group // 2
    while size >= 1:
        take = (n // size) % 2
        carry = lax.fori_loop(0, take, lambda _, cr, done=done, size=size: body(done, size, cr), carry)
        done = done + take * size
        size //= 2
    return carry


def _select_kernel(qi_ref, w_ref, kidx_ref, bias_ref, keys_ref, *, tq, tk, topk, idx_scale):
    seq = kidx_ref.shape[0]
    j = pl.program_id(1)
    n_tiles = ((j + 1) * tq + tk - 1) // tk
    lane = lax.broadcasted_iota(I32, (1, tq), 1)
    limit = ((j * tq + lane) // CHUNK + 1) * CHUNK
    row_local = lax.broadcasted_iota(I32, (tk, tq), 0)

    def score_tiles(t0, cnt, carry):
        rows = cnt * tk
        r0 = pl.multiple_of(t0 * tk, tk)
        kb = kidx_ref[pl.ds(r0, rows), :]
        acc = jnp.zeros((rows, tq), F32)
        for h in range(IDX_HEADS):
            s = jnp.dot(kb, qi_ref[h * IDX_HEAD_DIM:(h + 1) * IDX_HEAD_DIM, :],
                        preferred_element_type=F32)
            acc = acc + jnp.maximum(s, 0.0) * w_ref[h:h + 1, :]
        sc = acc * idx_scale + 0.0
        bits = lax.bitcast_convert_type(sc, I32)
        key = jnp.where(bits < 0, bits ^ 0x7FFFFFFF, bits)
        rows_i = lax.broadcasted_iota(I32, (rows, tq), 0) + r0
        keys_ref[pl.ds(r0, rows), :] = jnp.where(rows_i < limit, key, INT_MIN)
        return carry

    _grouped_loop(n_tiles, 2, score_tiles, 0)

    def count(pred):
        def body(kt, acc):
            r0 = pl.multiple_of(kt * tk, tk)
            ones = jnp.where(pred(keys_ref[pl.ds(r0, tk), :], r0), 1, 0).astype(I32)
            return acc + jnp.sum(ones.reshape(tk // 8, 8, tq), axis=0)
        acc = lax.fori_loop(0, n_tiles, body, jnp.zeros((8, tq), I32))
        return jnp.sum(acc, axis=0, keepdims=True)

    def row_index(kk, r0):
        return lax.broadcasted_iota(I32, kk.shape, 0) + r0

    def bisect(i, carry):
        thr, c_ge = carry
        cand = thr + jnp.left_shift(jnp.int32(1), 31 - i)
        c = count(lambda kk, r0: kk >= cand)
        ok = c >= topk
        return jnp.where(ok, cand, thr), jnp.where(ok, c, c_ge)

    thr, c_ge = lax.fori_loop(0, 32, bisect, (jnp.full((1, tq), INT_MIN, I32),
                                              jnp.full((1, tq), topk + 1, I32)))
    real = thr > INT_MIN
    tie_excess = jnp.max(jnp.where(jnp.logical_and(real, c_ge > topk), 1, 0)) > 0

    def write_simple():
        thr_eff = jnp.where(real, thr, INT_MIN + 1)

        def write_tile(kt, carry):
            r0 = pl.multiple_of(kt * tk, tk)
            sel = keys_ref[pl.ds(r0, tk), :] >= thr_eff
            bias_ref[pl.ds(r0, tk), :] = jnp.where(sel, 0.0, NEG_BIG).astype(bias_ref.dtype)
            return carry

        lax.fori_loop(0, n_tiles, write_tile, 0)

    def write_ties():
        need = topk - count(lambda kk, r0: kk > thr)
        idx_bits = seq.bit_length()

        def bisect_idx(i, lim):
            cand = lim + jnp.left_shift(jnp.int32(1), idx_bits - 1 - i)
            c = count(lambda kk, r0: jnp.logical_and(kk == thr, row_index(kk, r0) < cand))
            return jnp.where(c <= need, cand, lim)

        lim = lax.fori_loop(0, idx_bits, bisect_idx, jnp.zeros((1, tq), I32))
        lim = jnp.where(real, lim, 0)

        def write_tile(kt, carry):
            r0 = pl.multiple_of(kt * tk, tk)
            kk = keys_ref[pl.ds(r0, tk), :]
            sel = jnp.logical_or(kk > thr, jnp.logical_and(kk == thr, row_local + r0 < lim))
            bias_ref[pl.ds(r0, tk), :] = jnp.where(sel, 0.0, NEG_BIG).astype(bias_ref.dtype)
            return carry

        lax.fori_loop(0, n_tiles, write_tile, 0)

    lax.cond(tie_excess, write_ties, write_simple)

    def fill_tile(kt, carry):
        r0 = pl.multiple_of(kt * tk, tk)
        bias_ref[pl.ds(r0, tk), :] = jnp.full((tk, tq), NEG_BIG, bias_ref.dtype)
        return carry

    lax.fori_loop(n_tiles, seq // tk, fill_tile, 0)


def _select_bias(qi_t, w_t, kidx, bsz, seq, tq, tk):
    nq = seq // tq
    topk = min(TOPK_MAX, seq // 4)
    idx_scale = (IDX_HEAD_DIM ** -0.5) * (IDX_HEADS ** -0.5)
    n_qi = qi_t.shape[1]
    return pl.pallas_call(
        functools.partial(_select_kernel, tq=tq, tk=tk, topk=topk, idx_scale=idx_scale),
        out_shape=jax.ShapeDtypeStruct((bsz, seq, seq), BF16),
        grid=(bsz, nq),
        in_specs=[pl.BlockSpec((None, n_qi, tq), lambda b, j: (b * nq + j, 0, 0)),
                  pl.BlockSpec((None, IDX_HEADS, tq), lambda b, j: (b * nq + j, 0, 0)),
                  pl.BlockSpec((None, seq, IDX_HEAD_DIM), lambda b, j: (b, 0, 0))],
        out_specs=pl.BlockSpec((None, seq, tq), lambda b, j: (b, 0, j)),
        scratch_shapes=[pltpu.VMEM((seq, tq), I32)],
        compiler_params=_cparams(("parallel", "arbitrary")),
        name="topk_select",
    )(qi_t, w_t, kidx)


def _attn_kernel(q_ref, k_ref, v_ref, bias_ref, gate_ref, o_ref, s_ref, mc_ref, m_ref, acc_ref,
                 *, tq, tk, heads, group):
    hd = ATTN_HEAD_DIM
    n_tiles = pl.program_id(2) + 1
    qs = [q_ref[i * hd:(i + 1) * hd, :] for i in range(heads)]

    def logits(t0, cnt, slot):
        rows = cnt * tk
        r0 = t0 * tk if isinstance(t0, int) else pl.multiple_of(t0 * tk, tk)
        b = bias_ref[pl.ds(r0, rows), :].astype(F32)
        for i in range(heads):
            kb = k_ref[pl.ds(r0, rows), i * hd:(i + 1) * hd]
            s = jnp.dot(kb, qs[i], preferred_element_type=F32) + b
            s_ref[slot, i, pl.ds(0, rows), :] = s
            mc_ref[slot, i] = jnp.max(s.reshape(rows // 8, 8, tq), axis=0)

    def absorb(t0, cnt, slot):
        rows = cnt * tk
        ones = jnp.ones((16, rows), BF16)
        for i in range(heads):
            m = m_ref[i]
            m_new = jnp.maximum(m, jnp.max(mc_ref[slot, i], axis=0, keepdims=True))
            p = jnp.exp2(s_ref[slot, i, pl.ds(0, rows), :] - m_new).astype(BF16)
            vt = jnp.concatenate([v_ref[t0 + u, i * hd:(i + 1) * hd, :] for u in range(cnt)], axis=1)
            lhs = jnp.concatenate([vt, ones], axis=0)
            acc_ref[i] = jnp.exp2(m - m_new) * acc_ref[i] + jnp.dot(lhs, p, preferred_element_type=F32)
            m_ref[i] = m_new

    m_ref[...] = jnp.full(m_ref.shape, NEG_BIG, F32)
    acc_ref[...] = jnp.zeros(acc_ref.shape, F32)
    n_full = n_tiles // group
    max_full = (k_ref.shape[0] // tk) // group

    for ch in range(max_full + 1):
        if ch < max_full:
            @pl.when(ch < n_full)
            def _(ch=ch):
                logits(ch * group, group, ch % 2)
                if ch >= 1:
                    absorb((ch - 1) * group, group, (ch - 1) % 2)
        if ch >= 1:
            @pl.when(ch == n_full)
            def _(ch=ch):
                absorb((ch - 1) * group, group, (ch - 1) % 2)

    done = n_full * group
    size = group // 2
    while size >= 1:
        take = (n_tiles // size) % 2

        @pl.when(take == 1)
        def _(done=done, size=size):
            logits(done, size, 0)
            absorb(done, size, 0)

        done = done + take * size
        size //= 2
    o = jnp.concatenate([acc_ref[i, 0:hd, :] / acc_ref[i, hd:hd + 1, :] for i in range(heads)], axis=0).T
    o_ref[...] = (o * _silu(gate_ref[...])).astype(o_ref.dtype)


def _attention(q_t, k, v_t, bias, gates, bsz, seq, tq, heads=2, group=8):
    tk = tq
    nq = seq // tq
    nkt = seq // tk
    d_attn = k.shape[2]
    hw = heads * ATTN_HEAD_DIM
    return pl.pallas_call(
        functools.partial(_attn_kernel, tq=tq, tk=tk, heads=heads, group=group),
        out_shape=jax.ShapeDtypeStruct((bsz * seq, d_attn), BF16),
        grid=(bsz, d_attn // hw, nq),
        in_specs=[pl.BlockSpec((None, hw, tq), lambda b, h, j: (b * nq + j, h, 0)),
                  pl.BlockSpec((None, seq, hw), lambda b, h, j: (b, 0, h)),
                  pl.BlockSpec((nkt, hw, tk), lambda b, h, j: (b, h, 0)),
                  pl.BlockSpec((None, seq, tq), lambda b, h, j: (b, 0, j)),
                  pl.BlockSpec((tq, hw), lambda b, h, j: (b * nq + j, h))],
        out_specs=pl.BlockSpec((tq, hw), lambda b, h, j: (b * nq + j, h)),
        scratch_shapes=[pltpu.VMEM((2, heads, group * tk, tq), F32),
                        pltpu.VMEM((2, heads, 8, tq), F32),
                        pltpu.VMEM((heads, 1, tq), F32),
                        pltpu.VMEM((heads, ATTN_HEAD_DIM + 16, tq), F32)],
        compiler_params=_cparams(("parallel", "parallel", "arbitrary")),
        name="masked_attention",
    )(q_t, k, v_t, bias, gates)


def _merge_kernel(ym_ref, ya_ref, wm_ref, wa_ref, gm_ref, ga_ref, bm_ref, ba_ref, o_ref):
    o_m = jnp.dot(ym_ref[...], wm_ref[...], preferred_element_type=F32)
    o_a = jnp.dot(ya_ref[...], wa_ref[...], preferred_element_type=F32)
    g_m = jax.nn.sigmoid(gm_ref[...] + bm_ref[...])
    g_a = jax.nn.sigmoid(ga_ref[...] + ba_ref[...])
    o_ref[...] = (g_m * o_m + g_a * o_a).astype(o_ref.dtype)


def _merge(y_m, y_a, w_m, w_a, gates, gate_bias, d_attn, tm=1024, tn=512):
    t, d_inner = y_m.shape
    d_model = w_m.shape[1]
    nb = d_model // tn
    off_m = d_attn // tn
    off_a = off_m + nb
    gb = gate_bias.astype(F32).reshape(1, N_BRANCHES * d_model)
    return pl.pallas_call(
        _merge_kernel,
        out_shape=jax.ShapeDtypeStruct((t, d_model), BF16),
        grid=(t // tm, nb),
        in_specs=[pl.BlockSpec((tm, d_inner), lambda i, j: (i, 0)),
                  pl.BlockSpec((tm, d_attn), lambda i, j: (i, 0)),
                  pl.BlockSpec((d_inner, tn), lambda i, j: (0, j)),
                  pl.BlockSpec((d_attn, tn), lambda i, j: (0, j)),
                  pl.BlockSpec((tm, tn), lambda i, j: (i, off_m + j)),
                  pl.BlockSpec((tm, tn), lambda i, j: (i, off_a + j)),
                  pl.BlockSpec((1, tn), lambda i, j: (0, j)),
                  pl.BlockSpec((1, tn), lambda i, j: (0, nb + j))],
        out_specs=pl.BlockSpec((tm, tn), lambda i, j: (i, j)),
        compiler_params=_cparams(("parallel", "arbitrary")),
        name="branch_merge",
    )(y_m, y_a, w_m, w_a, gates, gates, gb, gb)


def _out_kernel(m_ref, w_ref, x_ref, o_ref):
    o_ref[...] = x_ref[...] + jnp.dot(m_ref[...], w_ref[...], preferred_element_type=F32)


def _out_norm_kernel(m_ref, w_ref, x_ref, g_ref, o_ref):
    y = x_ref[...] + jnp.dot(m_ref[...], w_ref[...], preferred_element_type=F32)
    ms = jnp.mean(y * y, axis=-1, keepdims=True)
    o_ref[...] = y * lax.rsqrt(ms + EPS) * g_ref[...]


def _out_proj(merged, w_out, x2d, norm_g, tm=512):
    t, d = x2d.shape
    row = pl.BlockSpec((tm, d), lambda i: (i, 0))
    in_specs = [row, pl.BlockSpec((d, d), lambda i: (0, 0)), row]
    args = [merged, w_out, x2d]
    kern = _out_kernel
    if norm_g is not None:
        in_specs.append(pl.BlockSpec((1, d), lambda i: (0, 0)))
        args.append(norm_g.astype(F32).reshape(1, d))
        kern = _out_norm_kernel
    return pl.pallas_call(
        kern,
        out_shape=jax.ShapeDtypeStruct((t, d), F32),
        grid=(t // tm,),
        in_specs=in_specs,
        out_specs=row,
        compiler_params=_cparams(("parallel",)),
        name="out_proj",
    )(*args)


def kernel(x, w_in, conv_w, conv_b, dt_bias, a_log, d_skip, gn_m, gate_bias, w_branch, w_out,
           norm_in, norm_final):
    bsz, seq, d_model = x.shape
    depth = w_in.shape[0]
    d_inner = gn_m.shape[1]
    conv_dim = conv_w.shape[2]
    n_heads = dt_bias.shape[1]
    d_attn = w_branch.shape[1] - d_inner
    n_qi = IDX_HEADS * IDX_HEAD_DIM
    t = bsz * seq
    tq_sel, tk_sel = 128, 512
    tq_att = 256
    tm, tn = 1024, 1024

    o_z = 0
    o_xbc = o_z + d_inner
    o_dt = o_xbc + conv_dim
    o_q = o_dt + n_heads
    o_k = o_q + d_attn
    o_v = o_k + d_attn
    o_ga = o_v + d_attn
    o_qi = o_ga + d_attn
    o_ki = o_qi + n_qi
    o_wi = o_ki + IDX_HEAD_DIM
    o_g = o_wi + IDX_HEADS
    small_pad = LANES - n_heads - IDX_HEAD_DIM

    x2d = x.reshape(t, d_model)
    for i in range(depth):
        w = w_in[i].astype(BF16)
        w_zx = w[:, o_z:o_dt]
        w_small = jnp.concatenate([w[:, o_dt:o_q], w[:, o_ki:o_wi],
                                   jnp.zeros((d_model, small_pad), BF16)], axis=1)
        w_k = w[:, o_k:o_v]
        w_gates = jnp.concatenate([w[:, o_ga:o_qi], w[:, o_g:]], axis=1)
        wt_q = w[:, o_q:o_k].T
        wt_v = w[:, o_v:o_ga].T
        wt_qi = w[:, o_qi:o_ki].T
        wt_wi = w[:, o_wi:o_g].T

        h = _rmsnorm(x2d, norm_in[i].astype(F32), BF16)
        zx = _matmul_nn(h, w_zx, F32, tm, tn, "proj_zx")
        small = _matmul_nn(h, w_small, F32, tm, LANES, "proj_small")
        k_tok = _matmul_nn(h, w_k, BF16, tm, tn, "proj_k")
        gates = _matmul_nn(h, w_gates, F32, tm, tn, "proj_gates")
        q_t = _matmul_nt(wt_q, h, BF16, tm, tn, tq_att, "proj_qT", scale=ATTN_LOGIT_SCALE)
        v_t = _matmul_nt(wt_v, h, BF16, tm, tn, tq_att, "proj_vT")
        qi_t = _matmul_nt(wt_qi, h, BF16, tm, tn, tq_sel, "proj_qidxT")
        wi_t = _matmul_nt(wt_wi, h, F32, tm, IDX_HEADS, tq_sel, "proj_widxT")

        y_m = _ssd_branch(zx, small, conv_w[i], conv_b[i], dt_bias[i], a_log[i], d_skip[i], gn_m[i],
                          bsz, seq)

        kidx = small[:, n_heads:n_heads + IDX_HEAD_DIM].astype(BF16).reshape(bsz, seq, IDX_HEAD_DIM)
        bias = _select_bias(qi_t, wi_t, kidx, bsz, seq, tq_sel, tk_sel)
        y_a = _attention(q_t, k_tok.reshape(bsz, seq, d_attn), v_t, bias, gates, bsz, seq, tq_att)

        wb = w_branch[i].astype(BF16)
        merged = _merge(y_m, y_a, wb[:d_inner], wb[d_inner:], gates, gate_bias[i], d_attn)
        last = i == depth - 1
        x2d = _out_proj(merged, w_out[i].astype(BF16), x2d, norm_final if last else None)
    return x2d.reshape(bsz, seq, d_model)
```

```python
import functools

import jax
import jax.numpy as jnp
from jax import lax
from jax.experimental import pallas as pl
from jax.experimental.pallas import tpu as pltpu

F32 = jnp.float32
BF16 = jnp.bfloat16
I32 = jnp.int32

EPS = 1e-6
CHUNK = 64
SSM_HEAD_DIM = 64
SSM_GROUPS = 4
SSM_STATE = 128
SSM_CONV = 4
ATTN_HEAD_DIM = 128
IDX_HEADS = 16
IDX_HEAD_DIM = 64
TOPK_MAX = 256
N_BRANCHES = 2
LOG2_E = 1.4426950408889634
ATTN_LOGIT_SCALE = (ATTN_HEAD_DIM ** -0.5) * LOG2_E

LANES = 128
NEG_BIG = -1e30
INT_MIN = -2 ** 31

VMEM_LIMIT = 56 * 1024 * 1024


def _cparams(sem):
    return pltpu.CompilerParams(dimension_semantics=sem, vmem_limit_bytes=VMEM_LIMIT)


def _rmsnorm_kernel(x_ref, g_ref, o_ref):
    x = x_ref[...]
    ms = jnp.mean(x * x, axis=-1, keepdims=True)
    o_ref[...] = (x * lax.rsqrt(ms + EPS) * g_ref[...]).astype(o_ref.dtype)


def _rmsnorm(x2d, g, out_dtype, tm=512):
    t, d = x2d.shape
    return pl.pallas_call(
        _rmsnorm_kernel,
        out_shape=jax.ShapeDtypeStruct((t, d), out_dtype),
        grid=(t // tm,),
        in_specs=[pl.BlockSpec((tm, d), lambda i: (i, 0)),
                  pl.BlockSpec((1, d), lambda i: (0, 0))],
        out_specs=pl.BlockSpec((tm, d), lambda i: (i, 0)),
        compiler_params=_cparams(("parallel",)),
        name="rmsnorm",
    )(x2d, g.reshape(1, d))


def _matmul_nn_kernel(a_ref, b_ref, o_ref):
    o_ref[...] = jnp.dot(a_ref[...], b_ref[...], preferred_element_type=F32).astype(o_ref.dtype)


def _matmul_nn(a, b, out_dtype, tm, tn, name):
    t, k = a.shape
    n = b.shape[1]
    return pl.pallas_call(
        _matmul_nn_kernel,
        out_shape=jax.ShapeDtypeStruct((t, n), out_dtype),
        grid=(t // tm, n // tn),
        in_specs=[pl.BlockSpec((tm, k), lambda i, j: (i, 0)),
                  pl.BlockSpec((k, tn), lambda i, j: (0, j))],
        out_specs=pl.BlockSpec((tm, tn), lambda i, j: (i, j)),
        compiler_params=_cparams(("parallel", "arbitrary")),
        name=name,
    )(a, b)


def _matmul_nt_kernel(w_ref, a_ref, o_ref, *, tt, scale):
    res = lax.dot_general(w_ref[...], a_ref[...], (((1,), (1,)), ((), ())),
                          preferred_element_type=F32)
    if scale is not None:
        res = res * scale
    res = res.astype(o_ref.dtype)
    for c in range(o_ref.shape[0]):
        o_ref[c] = res[:, c * tt:(c + 1) * tt]


def _matmul_nt(wt, a, out_dtype, tm, tn, tt, name, scale=None):
    t, k = a.shape
    n = wt.shape[0]
    return pl.pallas_call(
        functools.partial(_matmul_nt_kernel, tt=tt, scale=scale),
        out_shape=jax.ShapeDtypeStruct((t // tt, n, tt), out_dtype),
        grid=(t // tm, n // tn),
        in_specs=[pl.BlockSpec((tn, k), lambda i, j: (j, 0)),
                  pl.BlockSpec((tm, k), lambda i, j: (i, 0))],
        out_specs=pl.BlockSpec((tm // tt, tn, tt), lambda i, j: (i, j, 0)),
        compiler_params=_cparams(("parallel", "arbitrary")),
        name=name,
    )(wt, a)


def _softplus(x):
    return jnp.maximum(x, 0.0) + jnp.log1p(jnp.exp(-jnp.abs(x)))


def _silu(x):
    return x * (0.5 * jnp.tanh(0.5 * x) + 0.5)


def _ssd_kernel(zx_ref, dt_ref, convw_ref, convb_ref, dtb_ref, aneg_ref, dskip_ref, gnm_ref,
                o_ref, ext_ref, xs_ref, y_ref, state_ref, *, q, d_inner, conv_dim):
    n_heads = d_inner // SSM_HEAD_DIM
    n_pairs = n_heads // 2
    pairs_per_group = n_pairs // SSM_GROUPS
    halo = 8

    @pl.when(pl.program_id(1) == 0)
    def _():
        ext_ref[0:halo, :] = jnp.zeros((halo, conv_dim), F32)
        state_ref[...] = jnp.zeros(state_ref.shape, F32)

    ext_ref[halo:halo + q, :] = zx_ref[:, d_inner:d_inner + conv_dim]
    ct = 512
    for c0 in range(0, conv_dim, ct):
        base = halo - (SSM_CONV - 1)
        acc = ext_ref[base:base + q, c0:c0 + ct] * convw_ref[0:1, c0:c0 + ct]
        for kk in range(1, SSM_CONV):
            acc = acc + ext_ref[base + kk:base + kk + q, c0:c0 + ct] * convw_ref[kk:kk + 1, c0:c0 + ct]
        xs_ref[:, c0:c0 + ct] = _silu(acc + convb_ref[:, c0:c0 + ct])
    ext_ref[0:halo, :] = ext_ref[q:q + halo, :]

    dt = _softplus(dt_ref[...] + dtb_ref[...])
    a = dt * aneg_ref[...]
    row_i = lax.broadcasted_iota(I32, (q, q), 0)
    col_i = lax.broadcasted_iota(I32, (q, q), 1)
    tril = row_i >= col_i
    acum = jnp.dot(tril.astype(F32), a, precision=lax.Precision.HIGHEST,
                   preferred_element_type=F32) * LOG2_E
    acum_t = acum.T
    a_last = acum[q - 1:q, :]
    e_in = jnp.exp2(acum)
    e_out = jnp.exp2(a_last - acum)
    e_last = jnp.exp2(a_last)

    lane_lo = lax.broadcasted_iota(I32, (q, LANES), 1) < SSM_HEAD_DIM
    lane_lo_row = lax.broadcasted_iota(I32, (1, LANES), 1) < SSM_HEAD_DIM
    bc_off = d_inner
    cc_off = d_inner + SSM_GROUPS * SSM_STATE

    for g in range(SSM_GROUPS):
        bg = xs_ref[:, bc_off + g * SSM_STATE:bc_off + (g + 1) * SSM_STATE]
        cg = xs_ref[:, cc_off + g * SSM_STATE:cc_off + (g + 1) * SSM_STATE]
        bg16 = bg.astype(BF16)
        cg16 = cg.astype(BF16)
        cb = lax.dot_general(cg16, bg16, (((1,), (1,)), ((), ())), preferred_element_type=F32)
        bgt16 = bg.T.astype(BF16)
        for pi in range(pairs_per_group):
            pp = g * pairs_per_group + pi
            h0, h1 = 2 * pp, 2 * pp + 1
            x_pair = xs_ref[:, pp * LANES:(pp + 1) * LANES]
            xdt = x_pair * jnp.where(lane_lo, dt[:, h0:h0 + 1], dt[:, h1:h1 + 1])
            y_pair = dskip_ref[:, pp * LANES:(pp + 1) * LANES] * x_pair
            for hh, keep in ((h0, lane_lo), (h1, jnp.logical_not(lane_lo))):
                diff = acum[:, hh:hh + 1] - acum_t[hh:hh + 1, :]
                dec = jnp.exp2(jnp.where(tril, diff, NEG_BIG))
                m16 = (cb * dec).astype(BF16)
                xm16 = jnp.where(keep, xdt, 0.0).astype(BF16)
                y_pair = y_pair + jnp.dot(m16, xm16, preferred_element_type=F32)
            st = state_ref[pp]
            e_in_pair = jnp.where(lane_lo, e_in[:, h0:h0 + 1], e_in[:, h1:h1 + 1])
            y_pair = y_pair + jnp.dot(cg16, st.astype(BF16), preferred_element_type=F32) * e_in_pair
            e_out_pair = jnp.where(lane_lo, e_out[:, h0:h0 + 1], e_out[:, h1:h1 + 1])
            wx16 = (xdt * e_out_pair).astype(BF16)
            e_last_pair = jnp.where(lane_lo_row, e_last[:, h0:h0 + 1], e_last[:, h1:h1 + 1])
            state_ref[pp] = st * e_last_pair + jnp.dot(bgt16, wx16, preferred_element_type=F32)
            y_ref[:, pp * LANES:(pp + 1) * LANES] = y_pair

    gw = d_inner // SSM_GROUPS
    for g in range(SSM_GROUPS):
        z = zx_ref[:, g * gw:(g + 1) * gw]
        yg = y_ref[:, g * gw:(g + 1) * gw] * _silu(z)
        ms = jnp.mean(yg * yg, axis=-1, keepdims=True)
        o_ref[:, g * gw:(g + 1) * gw] = (yg * lax.rsqrt(ms + EPS) * gnm_ref[:, g * gw:(g + 1) * gw]).astype(o_ref.dtype)


def _ssd_branch(zx, dtk, conv_w, conv_b, dt_bias, a_log, d_skip, gn_m, bsz, seq, q=128):
    t = zx.shape[0]
    d_inner = gn_m.shape[0]
    conv_dim = conv_w.shape[1]
    n_heads = d_inner // SSM_HEAD_DIM
    nc = seq // q
    pad = LANES - n_heads
    dtb = jnp.pad(dt_bias.astype(F32), (0, pad)).reshape(1, LANES)
    aneg = jnp.pad(-jnp.exp(a_log.astype(F32)), (0, pad)).reshape(1, LANES)
    dskip = jnp.repeat(d_skip.astype(F32), SSM_HEAD_DIM).reshape(1, d_inner)
    const = lambda b, c: (0, 0)
    return pl.pallas_call(
        functools.partial(_ssd_kernel, q=q, d_inner=d_inner, conv_dim=conv_dim),
        out_shape=jax.ShapeDtypeStruct((t, d_inner), BF16),
        grid=(bsz, nc),
        in_specs=[pl.BlockSpec((q, d_inner + conv_dim), lambda b, c: (b * nc + c, 0)),
                  pl.BlockSpec((q, LANES), lambda b, c: (b * nc + c, 0)),
                  pl.BlockSpec((SSM_CONV, conv_dim), const),
                  pl.BlockSpec((1, conv_dim), const),
                  pl.BlockSpec((1, LANES), const),
                  pl.BlockSpec((1, LANES), const),
                  pl.BlockSpec((1, d_inner), const),
                  pl.BlockSpec((1, d_inner), const)],
        out_specs=pl.BlockSpec((q, d_inner), lambda b, c: (b * nc + c, 0)),
        scratch_shapes=[pltpu.VMEM((q + 8, conv_dim), F32),
                        pltpu.VMEM((q, conv_dim), F32),
                        pltpu.VMEM((q, d_inner), F32),
                        pltpu.VMEM((n_heads // 2, SSM_STATE, LANES), F32)],
        compiler_params=_cparams(("parallel", "arbitrary")),
        name="ssd_branch",
    )(zx, dtk, conv_w.astype(F32), conv_b.astype(F32).reshape(1, conv_dim), dtb, aneg, dskip,
      gn_m.astype(F32).reshape(1, d_inner))


def _grouped_loop(n, group, body, carry):
    carry = lax.fori_loop(0, n // group, lambda g, cr: body(g * group, group, cr), carry)
    done = (n // group) * group
    size = group // 2
    while size >= 1:
        take = (n // size) % 2
        carry = lax.fori_loop(0, take, lambda _, cr, done=done, size=size: body(done, size, cr), carry)
        done = done + take * size
        size //= 2
    return carry


def _select_kernel(qi_ref, w_ref, kidx_ref, bias_ref, keys_ref, *, tq, tk, topk, idx_scale):
    seq = kidx_ref.shape[0]
    j = pl.program_id(1)
    n_tiles = ((j + 1) * tq + tk - 1) // tk
    lane = lax.broadcasted_iota(I32, (1, tq), 1)
    limit = ((j * tq + lane) // CHUNK + 1) * CHUNK
    row_local = lax.broadcasted_iota(I32, (tk, tq), 0)

    def score_tiles(t0, cnt, carry):
        rows = cnt * tk
        r0 = pl.multiple_of(t0 * tk, tk)
        kb = kidx_ref[pl.ds(r0, rows), :]
        acc = jnp.zeros((rows, tq), F32)
        for h in range(IDX_HEADS):
            s = jnp.dot(kb, qi_ref[h * IDX_HEAD_DIM:(h + 1) * IDX_HEAD_DIM, :],
                        preferred_element_type=F32)
            acc = acc + jnp.maximum(s, 0.0) * w_ref[h:h + 1, :]
        sc = acc * idx_scale + 0.0
        bits = lax.bitcast_convert_type(sc, I32)
        key = jnp.where(bits < 0, bits ^ 0x7FFFFFFF, bits)
        rows_i = lax.broadcasted_iota(I32, (rows, tq), 0) + r0
        keys_ref[pl.ds(r0, rows), :] = jnp.where(rows_i < limit, key, INT_MIN)
        return carry

    _grouped_loop(n_tiles, 2, score_tiles, 0)

    def count(pred):
        def body(kt, acc):
            r0 = pl.multiple_of(kt * tk, tk)
            ones = jnp.where(pred(keys_ref[pl.ds(r0, tk), :], r0), 1, 0).astype(I32)
            return acc + jnp.sum(ones.reshape(tk // 8, 8, tq), axis=0)
        acc = lax.fori_loop(0, n_tiles, body, jnp.zeros((8, tq), I32))
        return jnp.sum(acc, axis=0, keepdims=True)

    def row_index(kk, r0):
        return lax.broadcasted_iota(I32, kk.shape, 0) + r0

    def bisect(i, carry):
        thr, c_ge = carry
        cand = thr + jnp.left_shift(jnp.int32(1), 31 - i)
        c = count(lambda kk, r0: kk >= cand)
        ok = c >= topk
        return jnp.where(ok, cand, thr), jnp.where(ok, c, c_ge)

    thr, c_ge = lax.fori_loop(0, 32, bisect, (jnp.full((1, tq), INT_MIN, I32),
                                              jnp.full((1, tq), topk + 1, I32)))
    real = thr > INT_MIN
    tie_excess = jnp.max(jnp.where(jnp.logical_and(real, c_ge > topk), 1, 0)) > 0

    def write_simple():
        thr_eff = jnp.where(real, thr, INT_MIN + 1)

        def write_tile(kt, carry):
            r0 = pl.multiple_of(kt * tk, tk)
            sel = keys_ref[pl.ds(r0, tk), :] >= thr_eff
            bias_ref[pl.ds(r0, tk), :] = jnp.where(sel, 0.0, NEG_BIG).astype(bias_ref.dtype)
            return carry

        lax.fori_loop(0, n_tiles, write_tile, 0)

    def write_ties():
        need = topk - count(lambda kk, r0: kk > thr)
        idx_bits = seq.bit_length()

        def bisect_idx(i, lim):
            cand = lim + jnp.left_shift(jnp.int32(1), idx_bits - 1 - i)
            c = count(lambda kk, r0: jnp.logical_and(kk == thr, row_index(kk, r0) < cand))
            return jnp.where(c <= need, cand, lim)

        lim = lax.fori_loop(0, idx_bits, bisect_idx, jnp.zeros((1, tq), I32))
        lim = jnp.where(real, lim, 0)

        def write_tile(kt, carry):
            r0 = pl.multiple_of(kt * tk, tk)
            kk = keys_ref[pl.ds(r0, tk), :]
            sel = jnp.logical_or(kk > thr, jnp.logical_and(kk == thr, row_local + r0 < lim))
            bias_ref[pl.ds(r0, tk), :] = jnp.where(sel, 0.0, NEG_BIG).astype(bias_ref.dtype)
            return carry

        lax.fori_loop(0, n_tiles, write_tile, 0)

    lax.cond(tie_excess, write_ties, write_simple)

    def fill_tile(kt, carry):
        r0 = pl.multiple_of(kt * tk, tk)
        bias_ref[pl.ds(r0, tk), :] = jnp.full((tk, tq), NEG_BIG, bias_ref.dtype)
        return carry

    lax.fori_loop(n_tiles, seq // tk, fill_tile, 0)


def _select_bias(qi_t, w_t, kidx, bsz, seq, tq, tk):
    nq = seq // tq
    topk = min(TOPK_MAX, seq // 4)
    idx_scale = (IDX_HEAD_DIM ** -0.5) * (IDX_HEADS ** -0.5)
    n_qi = qi_t.shape[1]
    return pl.pallas_call(
        functools.partial(_select_kernel, tq=tq, tk=tk, topk=topk, idx_scale=idx_scale),
        out_shape=jax.ShapeDtypeStruct((bsz, nq, seq, tq), BF16),
        grid=(bsz, nq),
        in_specs=[pl.BlockSpec((None, n_qi, tq), lambda b, j: (b * nq + j, 0, 0)),
                  pl.BlockSpec((None, IDX_HEADS, tq), lambda b, j: (b * nq + j, 0, 0)),
                  pl.BlockSpec((None, seq, IDX_HEAD_DIM), lambda b, j: (b, 0, 0))],
        out_specs=pl.BlockSpec((None, None, seq, tq), lambda b, j: (b, j, 0, 0)),
        scratch_shapes=[pltpu.VMEM((seq, tq), I32)],
        compiler_params=_cparams(("parallel", "arbitrary")),
        name="topk_select",
    )(qi_t, w_t, kidx)


def _attn_kernel(q_ref, k_ref, v_ref, bias_ref, gate_ref, o_ref, s_ref, mc_ref, m_ref, acc_ref,
                 *, tq, tk, heads, group):
    hd = ATTN_HEAD_DIM
    n_tiles = pl.program_id(2) + 1
    qs = [q_ref[i * hd:(i + 1) * hd, :] for i in range(heads)]

    def logits(t0, cnt, slot):
        rows = cnt * tk
        r0 = t0 * tk if isinstance(t0, int) else pl.multiple_of(t0 * tk, tk)
        b = jnp.concatenate([bias_ref[u, pl.ds(r0, rows), :] for u in range(bias_ref.shape[0])],
                            axis=1).astype(F32)
        for i in range(heads):
            kb = k_ref[pl.ds(r0, rows), i * hd:(i + 1) * hd]
            s = jnp.dot(kb, qs[i], preferred_element_type=F32) + b
            s_ref[slot, i, pl.ds(0, rows), :] = s
            mc_ref[slot, i] = jnp.max(s.reshape(rows // 8, 8, tq), axis=0)

    def absorb(t0, cnt, slot):
        rows = cnt * tk
        ones = jnp.ones((16, rows), BF16)
        for i in range(heads):
            m = m_ref[i]
            m_new = jnp.maximum(m, jnp.max(mc_ref[slot, i], axis=0, keepdims=True))
            p = jnp.exp2(s_ref[slot, i, pl.ds(0, rows), :] - m_new).astype(BF16)
            vt = jnp.concatenate([v_ref[t0 + u, i * hd:(i + 1) * hd, :] for u in range(cnt)], axis=1)
            lhs = jnp.concatenate([vt, ones], axis=0)
            acc_ref[i] = jnp.exp2(m - m_new) * acc_ref[i] + jnp.dot(lhs, p, preferred_element_type=F32)
            m_ref[i] = m_new

    m_ref[...] = jnp.full(m_ref.shape, NEG_BIG, F32)
    acc_ref[...] = jnp.zeros(acc_ref.shape, F32)
    n_full = n_tiles // group
    max_full = (k_ref.shape[0] // tk) // group

    for ch in range(max_full + 1):
        if ch < max_full:
            @pl.when(ch < n_full)
            def _(ch=ch):
                logits(ch * group, group, ch % 2)
                if ch >= 1:
                    absorb((ch - 1) * group, group, (ch - 1) % 2)
        if ch >= 1:
            @pl.when(ch == n_full)
            def _(ch=ch):
                absorb((ch - 1) * group, group, (ch - 1) % 2)

    done = n_full * group
    size = group // 2
    while size >= 1:
        take = (n_tiles // size) % 2

        @pl.when(take == 1)
        def _(done=done, size=size):
            logits(done, size, 0)
            absorb(done, size, 0)

        done = done + take * size
        size //= 2
    o = jnp.concatenate([acc_ref[i, 0:hd, :] / acc_ref[i, hd:hd + 1, :] for i in range(heads)], axis=0).T
    o_ref[...] = (o * _silu(gate_ref[...])).astype(o_ref.dtype)


def _attention(q_t, k, v_t, bias, gates, bsz, seq, tq, heads=2, group=8):
    tk = tq
    nq = seq // tq
    nkt = seq // tk
    d_attn = k.shape[2]
    hw = heads * ATTN_HEAD_DIM
    slabs = tq // bias.shape[3]
    return pl.pallas_call(
        functools.partial(_attn_kernel, tq=tq, tk=tk, heads=heads, group=group),
        out_shape=jax.ShapeDtypeStruct((bsz * seq, d_attn), BF16),
        grid=(bsz, d_attn // hw, nq),
        in_specs=[pl.BlockSpec((None, hw, tq), lambda b, h, j: (b * nq + j, h, 0)),
                  pl.BlockSpec((None, seq, hw), lambda b, h, j: (b, 0, h)),
                  pl.BlockSpec((nkt, hw, tk), lambda b, h, j: (b, h, 0)),
                  pl.BlockSpec((None, slabs, seq, bias.shape[3]), lambda b, h, j: (b, j, 0, 0)),
                  pl.BlockSpec((tq, hw), lambda b, h, j: (b * nq + j, h))],
        out_specs=pl.BlockSpec((tq, hw), lambda b, h, j: (b * nq + j, h)),
        scratch_shapes=[pltpu.VMEM((2, heads, group * tk, tq), F32),
                        pltpu.VMEM((2, heads, 8, tq), F32),
                        pltpu.VMEM((heads, 1, tq), F32),
                        pltpu.VMEM((heads, ATTN_HEAD_DIM + 16, tq), F32)],
        compiler_params=_cparams(("parallel", "parallel", "arbitrary")),
        name="masked_attention",
    )(q_t, k, v_t, bias, gates)


def _merge_kernel(ym_ref, ya_ref, wm_ref, wa_ref, gm_ref, ga_ref, bm_ref, ba_ref, o_ref):
    o_m = jnp.dot(ym_ref[...], wm_ref[...], preferred_element_type=F32)
    o_a = jnp.dot(ya_ref[...], wa_ref[...], preferred_element_type=F32)
    g_m = jax.nn.sigmoid(gm_ref[...] + bm_ref[...])
    g_a = jax.nn.sigmoid(ga_ref[...] + ba_ref[...])
    o_ref[...] = (g_m * o_m + g_a * o_a).astype(o_ref.dtype)


def _merge(y_m, y_a, w_m, w_a, gates, gate_bias, d_attn, tm=1024, tn=512):
    t, d_inner = y_m.shape
    d_model = w_m.shape[1]
    nb = d_model // tn
    off_m = d_attn // tn
    off_a = off_m + nb
    gb = gate_bias.astype(F32).reshape(1, N_BRANCHES * d_model)
    return pl.pallas_call(
        _merge_kernel,
        out_shape=jax.ShapeDtypeStruct((t, d_model), BF16),
        grid=(t // tm, nb),
        in_specs=[pl.BlockSpec((tm, d_inner), lambda i, j: (i, 0)),
                  pl.BlockSpec((tm, d_attn), lambda i, j: (i, 0)),
                  pl.BlockSpec((d_inner, tn), lambda i, j: (0, j)),
                  pl.BlockSpec((d_attn, tn), lambda i, j: (0, j)),
                  pl.BlockSpec((tm, tn), lambda i, j: (i, off_m + j)),
                  pl.BlockSpec((tm, tn), lambda i, j: (i, off_a + j)),
                  pl.BlockSpec((1, tn), lambda i, j: (0, j)),
                  pl.BlockSpec((1, tn), lambda i, j: (0, nb + j))],
        out_specs=pl.BlockSpec((tm, tn), lambda i, j: (i, j)),
        compiler_params=_cparams(("parallel", "arbitrary")),
        name="branch_merge",
    )(y_m, y_a, w_m, w_a, gates, gates, gb, gb)


def _out_kernel(m_ref, w_ref, x_ref, o_ref):
    o_ref[...] = x_ref[...] + jnp.dot(m_ref[...], w_ref[...], preferred_element_type=F32)


def _out_norm_kernel(m_ref, w_ref, x_ref, g_ref, o_ref):
    y = x_ref[...] + jnp.dot(m_ref[...], w_ref[...], preferred_element_type=F32)
    ms = jnp.mean(y * y, axis=-1, keepdims=True)
    o_ref[...] = y * lax.rsqrt(ms + EPS) * g_ref[...]


def _out_proj(merged, w_out, x2d, norm_g, tm=512):
    t, d = x2d.shape
    row = pl.BlockSpec((tm, d), lambda i: (i, 0))
    in_specs = [row, pl.BlockSpec((d, d), lambda i: (0, 0)), row]
    args = [merged, w_out, x2d]
    kern = _out_kernel
    if norm_g is not None:
        in_specs.append(pl.BlockSpec((1, d), lambda i: (0, 0)))
        args.append(norm_g.astype(F32).reshape(1, d))
        kern = _out_norm_kernel
    return pl.pallas_call(
        kern,
        out_shape=jax.ShapeDtypeStruct((t, d), F32),
        grid=(t // tm,),
        in_specs=in_specs,
        out_specs=row,
        compiler_params=_cparams(("parallel",)),
        name="out_proj",
    )(*args)


def kernel(x, w_in, conv_w, conv_b, dt_bias, a_log, d_skip, gn_m, gate_bias, w_branch, w_out,
           norm_in, norm_final):
    bsz, seq, d_model = x.shape
    depth = w_in.shape[0]
    d_inner = gn_m.shape[1]
    conv_dim = conv_w.shape[2]
    n_heads = dt_bias.shape[1]
    d_attn = w_branch.shape[1] - d_inner
    n_qi = IDX_HEADS * IDX_HEAD_DIM
    t = bsz * seq
    tq_sel, tk_sel = 128, 512
    tq_att = 256
    tm, tn = 1024, 1024

    o_z = 0
    o_xbc = o_z + d_inner
    o_dt = o_xbc + conv_dim
    o_q = o_dt + n_heads
    o_k = o_q + d_attn
    o_v = o_k + d_attn
    o_ga = o_v + d_attn
    o_qi = o_ga + d_attn
    o_ki = o_qi + n_qi
    o_wi = o_ki + IDX_HEAD_DIM
    o_g = o_wi + IDX_HEADS
    small_pad = LANES - n_heads - IDX_HEAD_DIM

    x2d = x.reshape(t, d_model)
    for i in range(depth):
        w = w_in[i].astype(BF16)
        w_zx = w[:, o_z:o_dt]
        w_small = jnp.concatenate([w[:, o_dt:o_q], w[:, o_ki:o_wi],
                                   jnp.zeros((d_model, small_pad), BF16)], axis=1)
        w_k = w[:, o_k:o_v]
        w_gates = jnp.concatenate([w[:, o_ga:o_qi], w[:, o_g:]], axis=1)
        wt_q = w[:, o_q:o_k].T
        wt_v = w[:, o_v:o_ga].T
        wt_qi = w[:, o_qi:o_ki].T
        wt_wi = w[:, o_wi:o_g].T

        h = _rmsnorm(x2d, norm_in[i].astype(F32), BF16)
        zx = _matmul_nn(h, w_zx, F32, tm, tn, "proj_zx")
        small = _matmul_nn(h, w_small, F32, tm, LANES, "proj_small")
        k_tok = _matmul_nn(h, w_k, BF16, tm, tn, "proj_k")
        gates = _matmul_nn(h, w_gates, F32, tm, tn, "proj_gates")
        q_t = _matmul_nt(wt_q, h, BF16, tm, tn, tq_att, "proj_qT", scale=ATTN_LOGIT_SCALE)
        v_t = _matmul_nt(wt_v, h, BF16, tm, tn, tq_att, "proj_vT")
        qi_t = _matmul_nt(wt_qi, h, BF16, tm, tn, tq_sel, "proj_qidxT")
        wi_t = _matmul_nt(wt_wi, h, F32, tm, IDX_HEADS, tq_sel, "proj_widxT")

        y_m = _ssd_branch(zx, small, conv_w[i], conv_b[i], dt_bias[i], a_log[i], d_skip[i], gn_m[i],
                          bsz, seq)

        kidx = small[:, n_heads:n_heads + IDX_HEAD_DIM].astype(BF16).reshape(bsz, seq, IDX_HEAD_DIM)
        bias = _select_bias(qi_t, wi_t, kidx, bsz, seq, tq_sel, tk_sel)
        y_a = _attention(q_t, k_tok.reshape(bsz, seq, d_attn), v_t, bias, gates, bsz, seq, tq_att)

        wb = w_branch[i].astype(BF16)
        merged = _merge(y_m, y_a, wb[:d_inner], wb[d_inner:], gates, gate_bias[i], d_attn)
        last = i == depth - 1
        x2d = _out_proj(merged, w_out[i].astype(BF16), x2d, norm_final if last else None)
    return x2d.reshape(bsz, seq, d_model)
```
